```python
import math
import jax, jax.numpy as jnp
from jax import lax
import numpy as np

D_MODEL = 1024
BATCH = 8
SEQ = 4096
DEPTH = 2

CTX_LEN = 256
GRID_W = 64
EPS = 1e-6
N_BRANCH = 3
ATT_HEADS = 8
ATT_HD = 64
ATT_VD = 2 * ATT_HD
Q_BLOCK = 128
ROPE_THETA = 10000.0
HG_HEADS = 8
HG_DK = 128
HG_DV = 128
GLA_CHUNK = 32
SSM_D_INNER = 2 * D_MODEL
SSM_HEADDIM = 64
SSM_HEADS = SSM_D_INNER // SSM_HEADDIM
SSM_GROUPS = 4
SSM_STATE = 128
SSM_CONV = 5
SSD_CHUNK = 64
SSM_XBC = SSM_D_INNER + 2 * SSM_GROUPS * SSM_STATE
FFN_HIDDEN = -(-8 * D_MODEL // (3 * 256)) * 256

ATT_QK = ATT_HEADS * 2 * ATT_HD
ATT_V = ATT_HEADS * ATT_VD
HG_K = HG_HEADS * HG_DK
HG_V = HG_HEADS * HG_DV
STATE_SPLITS = (ATT_QK, ATT_V, HG_K, HG_K, HG_V, SSM_XBC, SSM_HEADS, SSM_HEADS)
QUERY_SPLITS = (ATT_QK, HG_K, HG_V, SSM_D_INNER, N_BRANCH * D_MODEL)
ALL_SPLITS = STATE_SPLITS + QUERY_SPLITS
N_STATE_COLS = sum(STATE_SPLITS)
N_IN_COLS = sum(ALL_SPLITS)

kernel_name = 'hybrid_gated_diffattn_hgrn2_ssd_dit'


def rmsnorm(x, g):
    xf = x.astype(jnp.float32)
    y = xf * lax.rsqrt(jnp.mean(xf * xf, axis=-1, keepdims=True) + EPS)
    return (y * g.astype(jnp.float32)).astype(x.dtype)


def adaln(x, g, shift, scale):
    return rmsnorm(x, g) * (1 + scale) + shift


def split_cols(t, sizes):
    return jnp.split(t, np.cumsum(sizes)[:-1].tolist(), axis=-1)


def flip_seq(t, rev):
    return jnp.flip(t, axis=1) if rev else t


def rope_axis(t, pos):
    half = t.shape[-1] // 2
    freqs = ROPE_THETA ** (-jnp.arange(half, dtype=jnp.float32) / half)
    ang = pos[:, None] * freqs[None, :]
    shape = (1, pos.shape[0]) + (1,) * (t.ndim - 3) + (half,)
    cos, sin = jnp.cos(ang).reshape(shape), jnp.sin(ang).reshape(shape)
    t1, t2 = t[..., :half], t[..., half:]
    return jnp.concatenate([t1 * cos - t2 * sin, t2 * cos + t1 * sin], axis=-1)


def rope_2d(t, row, col):
    half = t.shape[-1] // 2
    return jnp.concatenate([rope_axis(t[..., :half], row), rope_axis(t[..., half:], col)], axis=-1).astype(t.dtype)


def diff_attn(q, k, v, lam):
    Bn, Lq = q.shape[:2]
    qb = q.reshape((Bn, Lq // Q_BLOCK, Q_BLOCK) + q.shape[2:]).swapaxes(0, 1)
    scale = ATT_HD ** -0.5

    def block(q_blk):
        s = jnp.einsum('bqhcd,bkhcd->bhcqk', q_blk, k).astype(jnp.float32) * scale
        p = jax.nn.softmax(s, axis=-1)
        w = (p[:, :, 0] - lam * p[:, :, 1]).astype(v.dtype)
        return jnp.einsum('bhqk,bkhe->bqhe', w, v)

    o = lax.map(block, qb)
    return o.swapaxes(0, 1).reshape((Bn, Lq) + o.shape[3:])


def hgrn_gates(f_raw, lb):
    fr = f_raw.astype(jnp.float32)
    log_f = jnp.log(lb + (1.0 - lb) * jax.nn.sigmoid(fr))
    k = (1.0 - lb) * jax.nn.sigmoid(-fr)
    return log_f, k


def gla_chunked(q, k, v, log_f, s0):
    Bn, L, H, _ = q.shape
    nc = L // GLA_CHUNK

    def chunks(t):
        return t.astype(jnp.float32).reshape(Bn, nc, GLA_CHUNK, H, t.shape[-1]).transpose(1, 0, 3, 2, 4)

    lower = jnp.tril(jnp.ones((GLA_CHUNK, GLA_CHUNK), dtype=bool))[:, :, None]

    def step(S, inp):
        qc, kc, vc, lfc = inp
        b = jnp.cumsum(lfc, axis=2)
        diff = b[:, :, :, None, :] - b[:, :, None, :, :]
        seg = jnp.where(lower, jnp.exp(jnp.where(lower, diff, 0.0)), 0.0)
        scores = jnp.einsum('bhtd,bhsd,bhtsd->bhts', qc, kc, seg)
        o = scores @ vc + jnp.einsum('bhtd,bhde->bhte', qc * jnp.exp(b), S)
        b_end = b[:, :, -1:, :]
        S = jnp.exp(b_end[:, :, 0, :, None]) * S + jnp.einsum('bhsd,bhse->bhde', kc * jnp.exp(b_end - b), vc)
        return S, o

    S, o = lax.scan(step, s0, (chunks(q), chunks(k), chunks(v), chunks(log_f)))
    return o.transpose(1, 0, 3, 2, 4).reshape(Bn, L, H, -1).astype(v.dtype), S


def gla_state(k, v, log_f):
    b = jnp.cumsum(log_f, axis=1)
    w = jnp.exp(b[:, -1:] - b)
    return jnp.einsum('blhd,blhe->bhde', k.astype(jnp.float32) * w, v.astype(jnp.float32))


def seg_decay(a):
    T = a.shape[-1]
    strict = jnp.tril(jnp.ones((T, T), dtype=bool), -1)
    lower = jnp.tril(jnp.ones((T, T), dtype=bool))
    cs = jnp.cumsum(jnp.where(strict, a[..., :, None], 0.0), axis=-2)
    return jnp.where(lower, jnp.exp(jnp.where(lower, cs, 0.0)), 0.0)


def ssd_chunked(xdt, a, Bm, Cm, s0):
    out_dtype = Cm.dtype
    Bn, L, H, P = xdt.shape
    G, N = Bm.shape[2], Bm.shape[3]
    R = H // G
    nc = L // SSD_CHUNK
    T = SSD_CHUNK
    xs = xdt.reshape(Bn, nc, T, G, R, P).swapaxes(0, 1)
    As = a.reshape(Bn, nc, T, G, R).transpose(1, 0, 3, 4, 2)
    Bs = Bm.astype(jnp.float32).reshape(Bn, nc, T, G, N).swapaxes(0, 1)
    Cs = Cm.astype(jnp.float32).reshape(Bn, nc, T, G, N).swapaxes(0, 1)

    def step(S, inp):
        xc, ac, bc, cc = inp
        acum = jnp.cumsum(ac, axis=-1)
        Lm = seg_decay(ac)
        y = (jnp.einsum('btgn,bsgn,bgrts,bsgrp->btgrp', cc, bc, Lm, xc)
             + jnp.einsum('btgn,bgrpn,bgrt->btgrp', cc, S, jnp.exp(acum)))
        S = (jnp.exp(acum[..., -1])[..., None, None] * S
             + jnp.einsum('bsgn,bgrs,bsgrp->bgrpn', bc, jnp.exp(acum[..., -1:] - acum), xc))
        return S, y

    S, y = lax.scan(step, s0.reshape(Bn, G, R, P, N), (xs, As, Bs, Cs))
    return y.swapaxes(0, 1).reshape(Bn, L, H, P).astype(out_dtype), S.reshape(Bn, H, P, N)


def ssd_state(xdt, a, Bm):
    Bn, L, H, P = xdt.shape
    G, N = Bm.shape[2], Bm.shape[3]
    acum = jnp.cumsum(a, axis=1)
    w = jnp.exp(acum[:, -1:] - acum)
    xw = (xdt * w[..., None]).reshape(Bn, L, G, H // G, P)
    return jnp.einsum('blgrp,blgn->bgrpn', xw, Bm.astype(jnp.float32)).reshape(Bn, H, P, N)


def dwconv(u, w, b):
    pad = SSM_CONV // 2
    y = lax.conv_general_dilated(u, w.astype(u.dtype)[:, None, :], window_strides=(1,),
                                 padding=[(pad, pad)], dimension_numbers=('NWC', 'WIO', 'NWC'),
                                 feature_group_count=u.shape[-1])
    return y + b.astype(u.dtype)


def ssm_conv_split(u, w, b):
    Bn, L = u.shape[:2]
    xs, bm, cm = split_cols(jax.nn.silu(dwconv(u, w, b)), (SSM_D_INNER, SSM_GROUPS * SSM_STATE, SSM_GROUPS * SSM_STATE))
    return (xs.reshape(Bn, L, SSM_HEADS, SSM_HEADDIM), bm.reshape(Bn, L, SSM_GROUPS, SSM_STATE),
            cm.reshape(Bn, L, SSM_GROUPS, SSM_STATE))


def att_post(o, g, lam_init):
    return (rmsnorm(o, g) * (1.0 - lam_init)).reshape(o.shape[0], o.shape[1], -1)


def hg_post(o, gate, g):
    return (rmsnorm(o, g) * jax.nn.silu(gate).reshape(o.shape)).reshape(o.shape[0], o.shape[1], -1)


def ssm_post(y, xs, z, d_skip, g):
    Bn, L = y.shape[:2]
    y = (y + d_skip[:, None] * xs).reshape(Bn, L, SSM_D_INNER)
    return rmsnorm(y * jax.nn.silu(z), g)


def merge_branches(o_att, o_hg, o_ssm, gates, w_b_att, w_b_hg, w_b_ssm, w_out):
    g_att, g_hg, g_ssm = jnp.split(jax.nn.sigmoid(gates), N_BRANCH, axis=-1)
    m = g_att * (o_att @ w_b_att) + g_hg * (o_hg @ w_b_hg) + g_ssm * (o_ssm @ w_b_ssm)
    return m @ w_out


def ffn(h, w1, w3, w2):
    return (jax.nn.silu(h @ w1) * (h @ w3)) @ w2


def layer(x, xc, mod, mod_c, row, col, l, last, att_lambda, lb, w_in, norm1_g, att_norm_g,
          hg_norm_g, conv_w, conv_b, dt_bias, a_log, d_skip, ssm_norm_g, w_b_att, w_b_hg,
          w_b_ssm, w_out, norm2_g, ffn_w1, ffn_w3, ffn_w2):
    f32 = jnp.float32
    Bn, L, _ = x.shape
    Lc = xc.shape[1]
    sh_m, sc_m, g_m, sh_f, sc_f, g_f = jnp.split(mod[:, None, :], 6, axis=-1)
    shc_m, scc_m, gc_m, shc_f, scc_f, gc_f = jnp.split(mod_c, 6, axis=-1)

    lam_init = 0.8 - 0.6 * math.exp(-0.3 * l)
    lq1, lk1, lq2, lk2 = att_lambda.astype(f32)
    lam = jnp.exp(jnp.sum(lq1 * lk1)) - jnp.exp(jnp.sum(lq2 * lk2)) + lam_init

    h = adaln(x, norm1_g, sh_m, sc_m)
    hc = adaln(xc, norm1_g, shc_m, scc_m)
    ak, av, hf_f, hf_b, hi, xbc, dt_f, dt_b, aq, hq, hgate, z, gates = split_cols(h @ w_in, ALL_SPLITS)
    if last:
        cc = split_cols(hc @ w_in[:, :N_STATE_COLS], STATE_SPLITS)
    else:
        cc = split_cols(hc @ w_in, ALL_SPLITS)
    akc, avc, hfc_f, hfc_b, hic, xbcc, dtc_f, dtc_b = cc[:8]

    k_ctx = akc.reshape(Bn, Lc, ATT_HEADS, 2, ATT_HD)
    v_ctx = avc.reshape(Bn, Lc, ATT_HEADS, ATT_VD)
    q_lat = rope_2d(aq.reshape(Bn, L, ATT_HEADS, 2, ATT_HD), row, col)
    k_all = jnp.concatenate([k_ctx, rope_2d(ak.reshape(Bn, L, ATT_HEADS, 2, ATT_HD), row, col)], axis=1)
    v_all = jnp.concatenate([v_ctx, av.reshape(Bn, L, ATT_HEADS, ATT_VD)], axis=1)
    o_att = diff_attn(q_lat, k_all, v_all, lam)

    q_hg = jax.nn.silu(hq).reshape(Bn, L, HG_HEADS, HG_DK)
    v_hg = hi.reshape(Bn, L, HG_HEADS, HG_DV)
    vc_hg = hic.reshape(Bn, Lc, HG_HEADS, HG_DV)
    o_dirs, oc_dirs = [], []
    for d, (f_lat, f_ctx) in enumerate(((hf_f, hfc_f), (hf_b, hfc_b))):
        rev = d == 1
        lb_d = lb[d].reshape(HG_HEADS, HG_DK)
        lf, kf = hgrn_gates(f_lat.reshape(Bn, L, HG_HEADS, HG_DK), lb_d)
        lfc, kfc = hgrn_gates(f_ctx.reshape(Bn, Lc, HG_HEADS, HG_DK), lb_d)
        if last:
            s_ctx = gla_state(flip_seq(kfc, rev), flip_seq(vc_hg, rev), flip_seq(lfc, rev))
        else:
            qc_hg = jax.nn.silu(cc[9]).reshape(Bn, Lc, HG_HEADS, HG_DK)
            oc, s_ctx = gla_chunked(flip_seq(qc_hg, rev), flip_seq(kfc, rev), flip_seq(vc_hg, rev),
                                    flip_seq(lfc, rev), jnp.zeros((Bn, HG_HEADS, HG_DK, HG_DV), f32))
            oc_dirs.append(flip_seq(oc, rev))
        o, _ = gla_chunked(flip_seq(q_hg, rev), flip_seq(kf, rev), flip_seq(v_hg, rev), flip_seq(lf, rev), s_ctx)
        o_dirs.append(flip_seq(o, rev))

    xs, bm, cm = ssm_conv_split(xbc, conv_w, conv_b)
    xsc, bmc, cmc = ssm_conv_split(xbcc, conv_w, conv_b)
    y_dirs, yc_dirs = [], []
    for d, (dt_lat, dt_ctx) in enumerate(((dt_f, dtc_f), (dt_b, dtc_b))):
        rev = d == 1
        A = -jnp.exp(a_log[d].astype(f32))
        dt = jax.nn.softplus(dt_lat.astype(f32) + dt_bias[d])
        dtc = jax.nn.softplus(dt_ctx.astype(f32) + dt_bias[d])
        xdt, a = xs.astype(f32) * dt[..., None], dt * A
        xdtc, ac = xsc.astype(f32) * dtc[..., None], dtc * A
        if last:
            s_ctx = ssd_state(flip_seq(xdtc, rev), flip_seq(ac, rev), flip_seq(bmc, rev))
        else:
            yc, s_ctx = ssd_chunked(flip_seq(xdtc, rev), flip_seq(ac, rev), flip_seq(bmc, rev), flip_seq(cmc, rev),
                                    jnp.zeros((Bn, SSM_HEADS, SSM_HEADDIM, SSM_STATE), f32))
            yc_dirs.append(flip_seq(yc, rev))
        y, _ = ssd_chunked(flip_seq(xdt, rev), flip_seq(a, rev), flip_seq(bm, rev), flip_seq(cm, rev), s_ctx)
        y_dirs.append(flip_seq(y, rev))

    mix = merge_branches(att_post(o_att, att_norm_g, lam_init),
                         hg_post(o_dirs[0] + o_dirs[1], hgate, hg_norm_g),
                         ssm_post(y_dirs[0] + y_dirs[1], xs, z, d_skip, ssm_norm_g),
                         gates, w_b_att, w_b_hg, w_b_ssm, w_out)
    x = x + g_m * mix
    x = x + g_f * ffn(adaln(x, norm2_g, sh_f, sc_f), ffn_w1, ffn_w3, ffn_w2)
    if last:
        return x, None

    oc_att = diff_attn(cc[8].reshape(Bn, Lc, ATT_HEADS, 2, ATT_HD), k_ctx, v_ctx, lam)
    mix_c = merge_branches(att_post(oc_att, att_norm_g, lam_init),
                           hg_post(oc_dirs[0] + oc_dirs[1], cc[10], hg_norm_g),
                           ssm_post(yc_dirs[0] + yc_dirs[1], xsc, cc[11], d_skip, ssm_norm_g),
                           cc[12], w_b_att, w_b_hg, w_b_ssm, w_out)
    xc = xc + gc_m * mix_c
    xc = xc + gc_f * ffn(adaln(xc, norm2_g, shc_f, scc_f), ffn_w1, ffn_w3, ffn_w2)
    return x, xc


def setup_inputs(seed: int = 0) -> dict:
    key = jax.random.key(seed)
    k = jax.random.split(key, 32)
    D = D_MODEL

    def nrm(i, shape, scale):
        return scale * jax.random.normal(k[i], shape, jnp.float32)

    dt = jnp.exp(jax.random.uniform(k[14], (DEPTH, 2, SSM_HEADS), jnp.float32, math.log(1e-3), math.log(1e-1)))
    return {
        'x': nrm(0, (BATCH, SEQ, D), 1.0),
        'c': nrm(1, (BATCH, D), 1.0),
        'ctx': nrm(2, (BATCH, CTX_LEN, D), 1.0),
        'c_ctx': nrm(3, (D,), 1.0),
        'w_ada': nrm(4, (DEPTH, D, 6 * D), 0.5 * D ** -0.5),
        'b_ada': nrm(5, (DEPTH, 6 * D), 0.01),
        'norm1_g': 1.0 + nrm(6, (DEPTH, D), 0.02),
        'w_in': nrm(7, (DEPTH, D, N_IN_COLS), D ** -0.5),
        'att_lambda': nrm(8, (DEPTH, 4, ATT_HD), 0.1),
        'att_norm_g': 1.0 + nrm(9, (DEPTH, ATT_VD), 0.02),
        'hg_lb_logits': nrm(10, (2, DEPTH, HG_K), 0.5),
        'hg_norm_g': 1.0 + nrm(11, (DEPTH, HG_DV), 0.02),
        'ssm_conv_w': nrm(12, (DEPTH, SSM_CONV, SSM_XBC), SSM_CONV ** -0.5),
        'ssm_conv_b': nrm(13, (DEPTH, SSM_XBC), 0.01),
        'ssm_dt_bias': dt + jnp.log(-jnp.expm1(-dt)),
        'ssm_a_log': jnp.log(jax.random.uniform(k[15], (DEPTH, 2, SSM_HEADS), jnp.float32, 1.0, 16.0)),
        'ssm_d': 1.0 + nrm(16, (DEPTH, SSM_HEADS), 0.1),
        'ssm_norm_g': 1.0 + nrm(17, (DEPTH, SSM_D_INNER), 0.02),
        'w_branch_att': nrm(18, (DEPTH, ATT_V, D), ATT_V ** -0.5),
        'w_branch_hg': nrm(19, (DEPTH, HG_V, D), HG_V ** -0.5),
        'w_branch_ssm': nrm(20, (DEPTH, SSM_D_INNER, D), SSM_D_INNER ** -0.5),
        'w_out': nrm(21, (DEPTH, D, D), D ** -0.5),
        'norm2_g': 1.0 + nrm(22, (DEPTH, D), 0.02),
        'ffn_w1': nrm(23, (DEPTH, D, FFN_HIDDEN), D ** -0.5),
        'ffn_w3': nrm(24, (DEPTH, D, FFN_HIDDEN), D ** -0.5),
        'ffn_w2': nrm(25, (DEPTH, FFN_HIDDEN, D), FFN_HIDDEN ** -0.5),
        'final_g': 1.0 + nrm(26, (D,), 0.02),
    }


def reference(x, c, ctx, c_ctx, w_ada, b_ada, norm1_g, w_in, att_lambda, att_norm_g, hg_lb_logits,
              hg_norm_g, ssm_conv_w, ssm_conv_b, ssm_dt_bias, ssm_a_log, ssm_d, ssm_norm_g,
              w_branch_att, w_branch_hg, w_branch_ssm, w_out, norm2_g, ffn_w1, ffn_w3, ffn_w2, final_g):
    L = x.shape[1]
    rows = L // GRID_W
    row = jnp.repeat(jnp.arange(rows, dtype=jnp.float32), GRID_W)
    col = jnp.broadcast_to(jnp.arange(GRID_W, dtype=jnp.float32), (rows, GRID_W)).reshape(-1)
    lb_w = jax.nn.softmax(hg_lb_logits.astype(jnp.float32), axis=1)
    lb = jnp.cumsum(lb_w, axis=1) - lb_w[:, :1]
    c_act = jax.nn.silu(c)
    cc_act = jax.nn.silu(c_ctx)
    xc = ctx
    for l in range(DEPTH):
        mod = c_act @ w_ada[l] + b_ada[l]
        mod_c = cc_act @ w_ada[l] + b_ada[l]
        x, xc = layer(x, xc, mod, mod_c, row, col, l, l == DEPTH - 1, att_lambda[l], lb[:, l], w_in[l],
                      norm1_g[l], att_norm_g[l], hg_norm_g[l], ssm_conv_w[l], ssm_conv_b[l], ssm_dt_bias[l],
                      ssm_a_log[l], ssm_d[l], ssm_norm_g[l], w_branch_att[l], w_branch_hg[l], w_branch_ssm[l],
                      w_out[l], norm2_g[l], ffn_w1[l], ffn_w3[l], ffn_w2[l])
    return rmsnorm(x, final_g)
```

```python
import functools
import math

import jax
import jax.numpy as jnp
from jax import lax
from jax.experimental import pallas as pl
from jax.experimental.pallas import tpu as pltpu

F32 = jnp.float32
BF16 = jnp.bfloat16

EPS = 1e-6
GRID_W = 64
ROPE_THETA = 10000.0
ATT_HEADS = 8
ATT_HD = 64
HG_HEADS = 8
SSM_HEADS = 32
SSM_HEADDIM = 64
SSM_GROUPS = 4
SSM_STATE = 128
SSM_CONV = 5
N_BRANCH = 3

LANES = 128
SUBLANES = 8
VMEM_LIMIT = 56 * 1024 * 1024

CHUNK = 128
DIAG = 8
LEVELS = (64, 32, 16, 8)


def _cparams(*sem):
    return pltpu.CompilerParams(dimension_semantics=sem, vmem_limit_bytes=VMEM_LIMIT)


def _sigmoid(x):
    return 1.0 / (1.0 + jnp.exp(-x))


def _silu(x):
    return x * _sigmoid(x)


def _split3(x):
    hi = x.astype(BF16)
    r = x - hi.astype(F32)
    mid = r.astype(BF16)
    lo = (r - mid.astype(F32)).astype(BF16)
    return hi, mid, lo


def _dot(a, b):
    return jnp.dot(a, b, preferred_element_type=F32)


def _dot_nt(a, b):
    return lax.dot_general(a, b, (((1,), (1,)), ((), ())), preferred_element_type=F32)


def _dot_tn(a, b):
    return lax.dot_general(a, b, (((0,), (0,)), ((), ())), preferred_element_type=F32)


def _mod_kernel(c_ref, w_ref, b_ref, o_ref):
    a = _silu(c_ref[...])
    a_hi = a.astype(BF16)
    a_lo = (a - a_hi.astype(F32)).astype(BF16)
    w = w_ref[...]
    w_hi = w.astype(BF16)
    w_lo = (w - w_hi.astype(F32)).astype(BF16)
    o_ref[...] = _dot(a_hi, w_hi) + _dot(a_lo, w_hi) + _dot(a_hi, w_lo) + b_ref[...]


def _mod_call(c_rows, w_ada, b_ada):
    depth, d, n = w_ada.shape
    r = c_rows.shape[0]
    tn = 1536
    return pl.pallas_call(
        _mod_kernel,
        grid=(depth, n // tn),
        in_specs=[
            pl.BlockSpec((r, d), lambda l, j: (0, 0)),
            pl.BlockSpec((None, d, tn), lambda l, j: (l, 0, j)),
            pl.BlockSpec((None, 1, tn), lambda l, j: (l, 0, j)),
        ],
        out_specs=pl.BlockSpec((None, r, tn), lambda l, j: (l, 0, j)),
        out_shape=jax.ShapeDtypeStruct((depth, r, n), F32),
        compiler_params=_cparams("parallel", "parallel"),
        name="mod",
    )(c_rows, w_ada, b_ada.reshape(depth, 1, n))


def _adaln_kernel(x_ref, g_ref, sh_ref, sc_ref, o_ref):
    x = x_ref[...]
    y = x * lax.rsqrt(jnp.mean(x * x, axis=-1, keepdims=True) + EPS) * g_ref[...]
    o_ref[...] = (y * (1.0 + sc_ref[...]) + sh_ref[...]).astype(o_ref.dtype)


def _adaln_call(x, g, shift, scale):
    bn, l, d = x.shape
    tl = min(l, 512)
    vec = pl.BlockSpec((None, 1, d), lambda b, i: (b, 0, 0))
    return pl.pallas_call(
        _adaln_kernel,
        grid=(bn, l // tl),
        in_specs=[
            pl.BlockSpec((None, tl, d), lambda b, i: (b, i, 0)),
            pl.BlockSpec((1, d), lambda b, i: (0, 0)),
            vec, vec,
        ],
        out_specs=pl.BlockSpec((None, tl, d), lambda b, i: (b, i, 0)),
        out_shape=jax.ShapeDtypeStruct((bn, l, d), BF16),
        compiler_params=_cparams("parallel", "parallel"),
        name="adaln",
    )(x, g.reshape(1, d), shift.reshape(bn, 1, d), scale.reshape(bn, 1, d))


def _proj_kernel(*refs, epi, scale):
    if epi == "rope":
        h_ref, w_ref, cos_ref, sin_ref, o_ref = refs
    else:
        h_ref, w_ref, o_ref = refs
    acc = _dot(h_ref[...], w_ref[...])
    if epi == "rope":
        cos = cos_ref[...]
        sin = sin_ref[...]
        for s in range(acc.shape[1] // LANES):
            t = acc[:, s * LANES:(s + 1) * LANES]
            r = t * cos + pltpu.roll(t, LANES // 2, 1) * sin
            o_ref[:, s * LANES:(s + 1) * LANES] = r.astype(o_ref.dtype)
    elif epi == "silu":
        o_ref[...] = _silu(acc).astype(o_ref.dtype)
    elif epi == "scale":
        o_ref[...] = (acc * scale).astype(o_ref.dtype)
    else:
        o_ref[...] = acc.astype(o_ref.dtype)


def _proj_call(h, w, out_dtype, epi="none", tables=None, scale=1.0, name="proj"):
    bn, l, k = h.shape
    n = w.shape[1]
    tm = min(l, 1024)
    tn = min(n, 1024)
    nl = l // tm
    in_specs = [
        pl.BlockSpec((None, tm, k), lambda b, i, j: (b, i, 0)),
        pl.BlockSpec((k, tn), lambda b, i, j: (0, j)),
    ]
    args = [h, w]
    if epi == "rope":
        tab = pl.BlockSpec((tm, LANES), lambda b, i, j: (i, 0))
        in_specs += [tab, tab]
        args += list(tables)
    return pl.pallas_call(
        functools.partial(_proj_kernel, epi=epi, scale=scale),
        grid=(bn, nl, n // tn),
        in_specs=in_specs,
        out_specs=pl.BlockSpec((None, tm, tn), lambda b, i, j: (b, i, j)),
        out_shape=jax.ShapeDtypeStruct((bn, l, n), out_dtype),
        compiler_params=_cparams("parallel", "parallel", "parallel"),
        name=name,
    )(*args)


def _conv_kernel(u_ref, w_ref, b_ref, o_ref, pad_ref, *, rows):
    l, cb = u_ref.shape
    halo = SUBLANES
    pad_ref[0:halo, :] = jnp.zeros((halo, cb), F32)
    pad_ref[halo + l:2 * halo + l, :] = jnp.zeros((halo, cb), F32)
    for r0 in range(0, l, rows):
        pad_ref[halo + r0:halo + r0 + rows, :] = u_ref[r0:r0 + rows, :].astype(F32)
    w = w_ref[...]
    bias = b_ref[...]
    half = SSM_CONV // 2
    for r0 in range(0, l, rows):
        acc = bias + w[0:1, :] * pad_ref[halo + r0 - half:halo + r0 - half + rows, :]
        for k in range(1, SSM_CONV):
            s = halo + r0 + k - half
            acc = acc + w[k:k + 1, :] * pad_ref[s:s + rows, :]
        o_ref[r0:r0 + rows, :] = _silu(acc).astype(o_ref.dtype)


def _conv_call(u, w, b):
    bn, l, c = u.shape
    cb = 512
    rows = min(l, 512)
    return pl.pallas_call(
        functools.partial(_conv_kernel, rows=rows),
        grid=(bn, c // cb),
        in_specs=[
            pl.BlockSpec((None, l, cb), lambda b_, j: (b_, 0, j)),
            pl.BlockSpec((SSM_CONV, cb), lambda b_, j: (0, j)),
            pl.BlockSpec((1, cb), lambda b_, j: (0, j)),
        ],
        out_specs=pl.BlockSpec((None, l, cb), lambda b_, j: (b_, 0, j)),
        out_shape=jax.ShapeDtypeStruct((bn, l, c), BF16),
        scratch_shapes=[pltpu.VMEM((l + 2 * SUBLANES, cb), F32)],
        compiler_params=_cparams("parallel", "parallel"),
        name="conv",
    )(u, w, b.reshape(1, c))


def _attn_kernel(*refs, nsrc, post_scale):
    lam_ref, q_ref = refs[0], refs[1]
    kv_refs = refs[2:2 + 2 * nsrc]
    g_ref, o_ref = refs[2 + 2 * nsrc], refs[3 + 2 * nsrc]
    q = q_ref[...]
    tq = q.shape[0]
    lane = lax.broadcasted_iota(jnp.int32, q.shape, 1)
    second_map = ((lane // 32) % 2) == 1
    zero = jnp.zeros_like(q)
    qq = jnp.concatenate([jnp.where(second_map, zero, q), jnp.where(second_map, q, zero)], axis=0)
    ss = [_dot_nt(qq, kv_refs[2 * i][...]) for i in range(nsrc)]
    m = jnp.max(ss[0], axis=1, keepdims=True)
    for s in ss[1:]:
        m = jnp.maximum(m, jnp.max(s, axis=1, keepdims=True))
    ps = [jnp.exp(s - m) for s in ss]
    den = jnp.sum(ps[0], axis=1, keepdims=True)
    for p in ps[1:]:
        den = den + jnp.sum(p, axis=1, keepdims=True)
    r = 1.0 / den
    c0 = r[:tq]
    c1 = lam_ref[0] * r[tq:]
    o = None
    for i in range(nsrc):
        wgt = (ps[i][:tq] * c0 - ps[i][tq:] * c1).astype(BF16)
        t = _dot(wgt, kv_refs[2 * i + 1][...])
        o = t if o is None else o + t
    o = o * lax.rsqrt(jnp.mean(o * o, axis=1, keepdims=True) + EPS) * (g_ref[...] * post_scale)
    o_ref[...] = o.astype(o_ref.dtype)


def _attn_call(lam, q, kvs, g, post_scale):
    bn, lq, hw = q.shape
    nh = hw // LANES
    tq = min(lq, 256)
    in_specs = [
        pl.BlockSpec(memory_space=pltpu.SMEM),
        pl.BlockSpec((None, tq, LANES), lambda b, h, i: (b, i, h)),
    ]
    args = [lam, q]
    for k, v, voff in kvs:
        lk = k.shape[1]
        in_specs += [pl.BlockSpec((None, lk, LANES), lambda b, h, i: (b, 0, h)),
                     pl.BlockSpec((None, lk, LANES), lambda b, h, i, voff=voff: (b, 0, h + voff))]
        args += [k, v]
    in_specs.append(pl.BlockSpec((1, LANES), lambda b, h, i: (0, 0)))
    args.append(g.reshape(1, LANES))
    return pl.pallas_call(
        functools.partial(_attn_kernel, nsrc=len(kvs), post_scale=post_scale),
        grid=(bn, nh, lq // tq),
        in_specs=in_specs,
        out_specs=pl.BlockSpec((None, tq, LANES), lambda b, h, i: (b, i, h)),
        out_shape=jax.ShapeDtypeStruct((bn, lq, hw), BF16),
        compiler_params=_cparams("parallel", "parallel", "parallel"),
        name="diff_attn",
    )(*args)


def _gla_masks(rev):
    ti = lax.broadcasted_iota(jnp.int32, (CHUNK, CHUNK), 0)
    si = lax.broadcasted_iota(jnp.int32, (CHUNK, CHUNK), 1)
    x = ti ^ si
    strict = (ti < si) if rev else (ti > si)
    incl = (ti <= si) if rev else (ti >= si)
    tri = jnp.where(incl, 1.0, 0.0).astype(BF16)
    lvl = [strict & (x >= h) & (x < 2 * h) for h in LEVELS]
    diag = incl & (x < DIAG)
    return tri, lvl, diag


def _gla_chunk(q, fr, v, lb, st_ref, b_ref, masks, rev, with_out):
    tri, lvl_masks, diag_mask = masks
    c = CHUNK
    sig = _sigmoid(fr)
    f = lb + (1.0 - lb) * sig
    lf = jnp.log(f)
    k = (1.0 - lb) * _sigmoid(-fr)
    hi, mid, lo = _split3(lf)
    b = _dot(tri, hi) + _dot(tri, mid) + _dot(tri, lo)
    b_ref[...] = b
    b_tot = b_ref[0:1, :] if rev else b_ref[c - 1:c, :]
    st = st_ref[...]
    out = None
    if with_out:
        qf = q.astype(F32)
        row = lax.broadcasted_iota(jnp.int32, (c, LANES), 0)
        a = jnp.zeros((c, c), F32)
        for h, msk in zip(LEVELS, lvl_masks):
            pieces = []
            for base in range(0, c, 2 * h):
                rr = base + h if rev else base + h - 1
                pieces.append(-jnp.abs(b_ref[base:base + 2 * h, :] - b_ref[rr:rr + 1, :]))
            e = jnp.exp(jnp.concatenate(pieces, axis=0) if len(pieces) > 1 else pieces[0])
            late = ((row & h) == 0) if rev else ((row & h) != 0)
            z = (jnp.where(late, qf, k) * e).astype(BF16)
            a = a + jnp.where(msk, _dot_nt(z, z), 0.0)
        pieces = []
        for base in range(0, c, DIAG):
            rr = base + DIAG - 1 if rev else base
            pieces.append(jnp.abs(b_ref[base:base + DIAG, :] - b_ref[rr:rr + 1, :]))
        d = jnp.concatenate(pieces, axis=0)
        qd = (qf * jnp.exp(-d)).astype(BF16)
        kd = (k * jnp.exp(d)).astype(BF16)
        a = a + jnp.where(diag_mask, _dot_nt(qd, kd), 0.0)
        out = _dot(a.astype(BF16), v) + _dot_nt((qf * jnp.exp(b)).astype(BF16), st.astype(BF16))
    kw = (k * jnp.exp(b_tot - b)).astype(BF16)
    st_ref[...] = st * jnp.exp(b_tot) + _dot_tn(v, kw)
    return out


def _gla_kernel(*refs, nc, with_out):
    if with_out:
        q_ref, frf_ref, frb_ref, v_ref, lbf_ref, lbb_ref, s0_ref, o_ref, st_out_ref, st_ref, b_ref = refs
    else:
        frf_ref, frb_ref, v_ref, lbf_ref, lbb_ref, s0_ref, st_out_ref, st_ref, b_ref = refs
        q_ref = o_ref = None
    masks_f = _gla_masks(False)
    masks_b = _gla_masks(True)
    st_ref[...] = s0_ref[...]
    if with_out:
        o_ref[...] = jnp.zeros(o_ref.shape, o_ref.dtype)

    def body(ci, carry):
        for d, (fr_ref, lb_ref, masks) in enumerate(((frf_ref, lbf_ref, masks_f), (frb_ref, lbb_ref, masks_b))):
            cc = ci if d == 0 else nc - 1 - ci
            rows = pl.ds(pl.multiple_of(cc * CHUNK, CHUNK), CHUNK)
            q = q_ref[rows, :] if with_out else None
            out = _gla_chunk(q, fr_ref[rows, :], v_ref[rows, :], lb_ref[...], st_ref.at[d], b_ref,
                             masks, d == 1, with_out)
            if with_out:
                o_ref[rows, :] += out
        return carry

    lax.fori_loop(0, nc, body, 0)
    st_out_ref[...] = st_ref[...]


def _gla_call(q, fr, v, voff, lb_f, lb_b, s0, with_out=True):
    bn, l, _ = v.shape
    nh = HG_HEADS
    hw = nh * LANES
    seq = lambda off: pl.BlockSpec((None, l, LANES), lambda b, h: (b, 0, h + off))
    lbs = pl.BlockSpec((1, LANES), lambda b, h: (0, h))
    sts = pl.BlockSpec((None, None, 2, LANES, LANES), lambda b, h: (b, h, 0, 0, 0))
    in_specs = [seq(0), seq(nh), seq(voff), lbs, lbs, sts]
    args = [fr, fr, v, lb_f, lb_b, s0]
    out_specs = [sts]
    out_shape = [jax.ShapeDtypeStruct(s0.shape, F32)]
    if with_out:
        in_specs = [seq(0)] + in_specs
        args = [q] + args
        out_specs = [seq(0)] + out_specs
        out_shape = [jax.ShapeDtypeStruct((bn, l, hw), F32)] + out_shape
    res = pl.pallas_call(
        functools.partial(_gla_kernel, nc=l // CHUNK, with_out=with_out),
        grid=(bn, nh),
        in_specs=in_specs,
        out_specs=out_specs,
        out_shape=out_shape,
        scratch_shapes=[pltpu.VMEM((2, LANES, LANES), F32), pltpu.VMEM((CHUNK, LANES), F32)],
        compiler_params=_cparams("parallel", "parallel"),
        name="hgrn2",
    )(*args)
    return (res[0], res[1]) if with_out else (None, res[0])


def _softplus(x):
    return jnp.maximum(x, 0.0) + jnp.log1p(jnp.exp(-jnp.abs(x)))


def _ssd_chunk(xs, bc, cc, cb, dt_col_raw, dt_row_raw, bias_row, bias_col, alog_row, alog_col,
               sn_ref, rev, with_out):
    t = CHUNK
    hp = xs.shape[1]
    npair = hp // LANES
    ti = lax.broadcasted_iota(jnp.int32, (t, t), 0)
    si = lax.broadcasted_iota(jnp.int32, (t, t), 1)
    incl = (ti <= si) if rev else (ti >= si)
    tri = jnp.where(incl, 1.0, 0.0).astype(BF16)
    lane = lax.broadcasted_iota(jnp.int32, (t, LANES), 1)
    first = lane < SSM_HEADDIM

    dt_col = _softplus(dt_col_raw + bias_row)
    a_col = dt_col * (-jnp.exp(alog_row))
    hi, mid, lo = _split3(a_col)
    acum_col = _dot(tri, hi) + _dot(tri, mid) + _dot(tri, lo)
    if with_out:
        dt_row = _softplus(dt_row_raw + bias_col)
        a_row = dt_row * (-jnp.exp(alog_col))
        hi, mid, lo = _split3(a_row)
        acum_row = _dot_nt(hi, tri) + _dot_nt(mid, tri) + _dot_nt(lo, tri)

    def pair_cols(m, i):
        return jnp.where(first, m[:, 2 * i:2 * i + 1], m[:, 2 * i + 1:2 * i + 2])

    sn = sn_ref[...]
    ys, xws, dec = [], [], []
    for i in range(npair):
        dt_p = pair_cols(dt_col, i)
        ac_p = pair_cols(acum_col, i)
        xdt = xs[:, i * LANES:(i + 1) * LANES].astype(F32) * dt_p
        atot = ac_p[0:1, :] if rev else ac_p[t - 1:t, :]
        xws.append((xdt * jnp.exp(atot - ac_p)).astype(BF16))
        dec.append(jnp.exp(atot))
        if with_out:
            ws = []
            for j in (2 * i, 2 * i + 1):
                diff = acum_col[:, j:j + 1] - acum_row[j:j + 1, :]
                lm = jnp.where(incl, jnp.exp(jnp.where(incl, diff, 0.0)), 0.0)
                ws.append((cb * lm).astype(BF16))
            lhs = jnp.concatenate(ws, axis=1)
            zero = jnp.zeros_like(xdt)
            rhs = jnp.concatenate([jnp.where(first, xdt, zero), jnp.where(first, zero, xdt)],
                                  axis=0).astype(BF16)
            y_state = _dot(cc, sn[:, i * LANES:(i + 1) * LANES].astype(BF16)) * jnp.exp(ac_p)
            ys.append(_dot(lhs, rhs) + y_state)
    xw = jnp.concatenate(xws, axis=1)
    sn_ref[...] = sn * jnp.concatenate(dec, axis=1) + _dot_tn(bc, xw)
    return jnp.concatenate(ys, axis=1) if with_out else None


def _ssd_kernel(*refs, nc, with_out):
    if with_out:
        (xs_ref, b_ref, c_ref, dtc_ref, dtr_ref, biasr_ref, biasc_ref, alogr_ref, alogc_ref,
         dskip_ref, s0_ref, y_ref, st_out_ref, sn_ref) = refs
    else:
        (xs_ref, b_ref, dtc_ref, biasr_ref, alogr_ref, s0_ref, st_out_ref, sn_ref) = refs
        c_ref = dtr_ref = biasc_ref = alogc_ref = dskip_ref = y_ref = None
    sn_ref[...] = s0_ref[...]
    if with_out:
        dsk = dskip_ref[...]

        def init(ci, carry):
            rows = pl.ds(pl.multiple_of(ci * CHUNK, CHUNK), CHUNK)
            y_ref[rows, :] = xs_ref[rows, :].astype(F32) * dsk
            return carry

        lax.fori_loop(0, nc, init, 0)

    def body(ci, carry):
        for d in (0, 1):
            cidx = ci if d == 0 else nc - 1 - ci
            start = pl.multiple_of(cidx * CHUNK, CHUNK)
            rows = pl.ds(start, CHUNK)
            xs = xs_ref[rows, :]
            bc = b_ref[rows, :]
            if with_out:
                cc = c_ref[rows, :]
                cb = _dot_nt(cc, bc)
                y = _ssd_chunk(xs, bc, cc, cb, dtc_ref[d, rows, :], dtr_ref[d, :, rows],
                               biasr_ref[d], biasc_ref[d], alogr_ref[d], alogc_ref[d],
                               sn_ref.at[d], d == 1, True)
                y_ref[rows, :] += y
            else:
                _ssd_chunk(xs, bc, None, None, dtc_ref[d, rows, :], None, biasr_ref[d], None,
                           alogr_ref[d], None, sn_ref.at[d], d == 1, False)
        return carry

    lax.fori_loop(0, nc, body, 0)
    st_out_ref[...] = sn_ref[...]


def _ssd_call(xbc, dt_col, dt_row, bias, alog, dskip, s0, with_out=True):
    bn, l, _ = xbc.shape
    g = SSM_GROUPS
    hg = SSM_HEADS // g
    hp = hg * SSM_HEADDIM
    d_inner = SSM_HEADS * SSM_HEADDIM
    nb_off = d_inner // LANES
    xs_spec = pl.BlockSpec((None, l, hp), lambda b, gi: (b, 0, gi))
    b_spec = pl.BlockSpec((None, l, LANES), lambda b, gi: (b, 0, nb_off + gi))
    c_spec = pl.BlockSpec((None, l, LANES), lambda b, gi: (b, 0, nb_off + g + gi))
    dtc_spec = pl.BlockSpec((None, None, 2, l, hg), lambda b, gi: (b, gi, 0, 0, 0))
    dtr_spec = pl.BlockSpec((None, None, 2, hg, l), lambda b, gi: (b, gi, 0, 0, 0))
    prow = pl.BlockSpec((None, 2, 1, hg), lambda b, gi: (gi, 0, 0, 0))
    pcol = pl.BlockSpec((None, 2, hg, 1), lambda b, gi: (gi, 0, 0, 0))
    dsk_spec = pl.BlockSpec((None, 1, hp), lambda b, gi: (gi, 0, 0))
    st_spec = pl.BlockSpec((None, None, 2, SSM_STATE, hp), lambda b, gi: (b, gi, 0, 0, 0))
    bias_r = bias.reshape(g, 2, 1, hg)
    bias_c = bias.reshape(g, 2, hg, 1)
    alog_r = alog.reshape(g, 2, 1, hg)
    alog_c = alog.reshape(g, 2, hg, 1)
    if with_out:
        in_specs = [xs_spec, b_spec, c_spec, dtc_spec, dtr_spec, prow, pcol, prow, pcol, dsk_spec, st_spec]
        args = [xbc, xbc, xbc, dt_col, dt_row, bias_r, bias_c, alog_r, alog_c, dskip, s0]
        out_specs = [xs_spec, st_spec]
        out_shape = [jax.ShapeDtypeStruct((bn, l, d_inner), F32), jax.ShapeDtypeStruct(s0.shape, F32)]
    else:
        in_specs = [xs_spec, b_spec, dtc_spec, prow, prow, st_spec]
        args = [xbc, xbc, dt_col, bias_r, alog_r, s0]
        out_specs = [st_spec]
        out_shape = [jax.ShapeDtypeStruct(s0.shape, F32)]
    res = pl.pallas_call(
        functools.partial(_ssd_kernel, nc=l // CHUNK, with_out=with_out),
        grid=(bn, g),
        in_specs=in_specs,
        out_specs=out_specs,
        out_shape=out_shape,
        scratch_shapes=[pltpu.VMEM((2, SSM_STATE, hp), F32)],
        compiler_params=_cparams("parallel", "parallel"),
        name="ssd",
    )(*args)
    return (res[0], res[1]) if with_out else (None, res[0])


def _merge_kernel(x_ref, oatt_ref, ohg_ref, hgate_ref, y_ref, z_ref, gates_ref, hgn_ref, ssn_ref,
                  gm_ref, wba_ref, wbh_ref, wbs_ref, wout_ref, o_ref):
    d = x_ref.shape[1]
    hgn = hgn_ref[...]
    parts = []
    for s in range(ohg_ref.shape[1] // LANES):
        sl = slice(s * LANES, (s + 1) * LANES)
        o = ohg_ref[:, sl]
        o = o * lax.rsqrt(jnp.mean(o * o, axis=1, keepdims=True) + EPS) * hgn
        parts.append((o * _silu(hgate_ref[:, sl].astype(F32))).astype(BF16))
    o_hg = jnp.concatenate(parts, axis=1)
    ys = y_ref[...] * _silu(z_ref[...].astype(F32))
    o_ssm = (ys * lax.rsqrt(jnp.mean(ys * ys, axis=1, keepdims=True) + EPS) * ssn_ref[...]).astype(BF16)
    g_att = _sigmoid(gates_ref[:, 0:d].astype(F32))
    g_hg = _sigmoid(gates_ref[:, d:2 * d].astype(F32))
    g_ssm = _sigmoid(gates_ref[:, 2 * d:3 * d].astype(F32))
    m = (g_att * _dot(oatt_ref[...], wba_ref[...]) + g_hg * _dot(o_hg, wbh_ref[...])
         + g_ssm * _dot(o_ssm, wbs_ref[...]))
    mix = _dot(m.astype(BF16), wout_ref[...])
    o_ref[...] = x_ref[...] + gm_ref[...] * mix


def _merge_call(x, o_att, o_hg, y, gz, hg_norm_g, ssm_norm_g, g_m, wba, wbh, wbs, wout):
    bn, l, d = x.shape
    tm = min(l, 256)
    di = y.shape[2]
    assert di == 2 * d and gz.shape[2] == 6 * d
    tok = lambda w, j=0: pl.BlockSpec((None, tm, w), lambda b, i: (b, i, j))
    full = lambda a: pl.BlockSpec(a.shape, lambda b, i: (0,) * a.ndim)
    hgn = hg_norm_g.reshape(1, LANES)
    ssn = ssm_norm_g.reshape(1, di)
    return pl.pallas_call(
        _merge_kernel,
        grid=(bn, l // tm),
        in_specs=[tok(d), tok(d), tok(d), tok(d, 2), tok(di), tok(di, 0), tok(N_BRANCH * d, 1), full(hgn), full(ssn),
                  pl.BlockSpec((None, 1, d), lambda b, i: (b, 0, 0)),
                  full(wba), full(wbh), full(wbs), full(wout)],
        out_specs=tok(d),
        out_shape=jax.ShapeDtypeStruct((bn, l, d), F32),
        compiler_params=_cparams("parallel", "parallel"),
        name="merge",
    )(x, o_att, o_hg, gz, y, gz, gz, hgn, ssn, g_m.reshape(bn, 1, d), wba, wbh, wbs, wout)


def _ffn_kernel(x_ref, g_ref, sh_ref, sc_ref, gf_ref, w1_ref, w3_ref, w2_ref, fin_ref, o_ref, *, final):
    x = x_ref[...]
    h = x * lax.rsqrt(jnp.mean(x * x, axis=-1, keepdims=True) + EPS) * g_ref[...]
    h = (h * (1.0 + sc_ref[...]) + sh_ref[...]).astype(BF16)
    u = _dot(h, w1_ref[...])
    t = _dot(h, w3_ref[...])
    a = (_silu(u) * t).astype(BF16)
    y = x + gf_ref[...] * _dot(a, w2_ref[...])
    if final:
        y = y * lax.rsqrt(jnp.mean(y * y, axis=-1, keepdims=True) + EPS) * fin_ref[...]
    o_ref[...] = y


def _ffn_call(x, g, shift, scale, gate, w1, w3, w2, final_g, final):
    bn, l, d = x.shape
    tm = min(l, 256)
    tok = pl.BlockSpec((None, tm, d), lambda b, i: (b, i, 0))
    vec = pl.BlockSpec((None, 1, d), lambda b, i: (b, 0, 0))
    row = pl.BlockSpec((1, d), lambda b, i: (0, 0))
    full = lambda a: pl.BlockSpec(a.shape, lambda b, i: (0, 0))
    return pl.pallas_call(
        functools.partial(_ffn_kernel, final=final),
        grid=(bn, l // tm),
        in_specs=[tok, row, vec, vec, vec, full(w1), full(w3), full(w2), row],
        out_specs=tok,
        out_shape=jax.ShapeDtypeStruct((bn, l, d), F32),
        compiler_params=_cparams("parallel", "parallel"),
        name="ffn",
    )(x, g.reshape(1, d), shift.reshape(bn, 1, d), scale.reshape(bn, 1, d), gate.reshape(bn, 1, d),
      w1, w3, w2, final_g.reshape(1, d))


def _rope_perm():
    idx = []
    for lane in range(LANES):
        half, mp, axis, i = lane // 64, (lane // 32) % 2, (lane // 16) % 2, lane % 16
        idx.append(mp * 64 + axis * 32 + half * 16 + i)
    return idx


def _rope_tables(l):
    rows = l // GRID_W
    row = jnp.repeat(jnp.arange(rows, dtype=F32), GRID_W)
    col = jnp.broadcast_to(jnp.arange(GRID_W, dtype=F32), (rows, GRID_W)).reshape(-1)
    nfreq = ATT_HD // 4
    freqs = ROPE_THETA ** (-jnp.arange(nfreq, dtype=F32) / nfreq)
    ang_r = row[:, None] * freqs[None, :]
    ang_c = col[:, None] * freqs[None, :]
    ang = jnp.concatenate([ang_r, ang_c, ang_r, ang_c], axis=1)
    ang = jnp.concatenate([ang, ang], axis=1)
    sign = jnp.where(jnp.arange(LANES) < LANES // 2, -1.0, 1.0).astype(F32)
    return jnp.cos(ang), jnp.sin(ang) * sign[None, :]


def _dt_layouts(dt_raw, bn, l):
    hg = SSM_HEADS // SSM_GROUPS
    t = dt_raw.reshape(bn, l, 2, SSM_GROUPS, hg)
    return t.transpose(0, 3, 2, 1, 4), t.transpose(0, 3, 2, 4, 1)


def kernel(x, c, ctx, c_ctx, w_ada, b_ada, norm1_g, w_in, att_lambda, att_norm_g, hg_lb_logits, hg_norm_g,
           ssm_conv_w, ssm_conv_b, ssm_dt_bias, ssm_a_log, ssm_d, ssm_norm_g, w_branch_att, w_branch_hg,
           w_branch_ssm, w_out, norm2_g, ffn_w1, ffn_w3, ffn_w2, final_g):
    bn, l, d = x.shape
    lc = ctx.shape[1]
    depth = w_ada.shape[0]
    att_w = ATT_HEADS * 2 * ATT_HD
    hg_w = HG_HEADS * LANES
    d_inner = SSM_HEADS * SSM_HEADDIM
    xbc_w = d_inner + 2 * SSM_GROUPS * SSM_STATE
    hgrp = SSM_HEADS // SSM_GROUPS

    sizes = (att_w, att_w, hg_w, hg_w, hg_w, xbc_w, SSM_HEADS, SSM_HEADS, att_w, hg_w, hg_w, d_inner, N_BRANCH * d)
    offs = [0]
    for s in sizes:
        offs.append(offs[-1] + s)
    (o_ak, o_av, o_hff, o_hfb, o_hi, o_xbc, o_dtf, o_dtb, o_aq, o_hq, o_hgate, o_z, o_gates, _) = offs

    perm = jnp.asarray([h * LANES + p for h in range(ATT_HEADS) for p in _rope_perm()], jnp.int32)
    cos_t, sin_t = _rope_tables(l)
    q_scale = ATT_HD ** -0.5

    lb_w = jax.nn.softmax(hg_lb_logits.astype(F32), axis=1)
    lb = jnp.cumsum(lb_w, axis=1) - lb_w[:, :1]

    pad_rows = (-(bn + 1)) % SUBLANES
    c_rows = jnp.concatenate([c, c_ctx[None, :], jnp.zeros((pad_rows, d), F32)], axis=0)
    mod_all = _mod_call(c_rows, w_ada, b_ada)

    xc = ctx
    for li in range(depth):
        last = li == depth - 1
        mod = mod_all[li, :bn]
        mod_c = jnp.broadcast_to(mod_all[li, bn:bn + 1], (bn, 6 * d))
        sh_m, sc_m, g_m, sh_f, sc_f, g_f = [mod[:, i * d:(i + 1) * d] for i in range(6)]
        shc_m, scc_m, gc_m, shc_f, scc_f, gc_f = [mod_c[:, i * d:(i + 1) * d] for i in range(6)]

        lam_init = 0.8 - 0.6 * math.exp(-0.3 * li)
        al = att_lambda[li].astype(F32)
        lam = (jnp.exp(jnp.sum(al[0] * al[1])) - jnp.exp(jnp.sum(al[2] * al[3])) + lam_init).reshape(1)

        w = w_in[li]
        wb = lambda o, n: w[:, o:o + n].astype(BF16)
        w_q = wb(o_aq, att_w)[:, perm]
        w_k = wb(o_ak, att_w)[:, perm]
        w_vv = jnp.concatenate([wb(o_av, att_w), wb(o_hi, hg_w)], axis=1)
        w_f = wb(o_hff, 2 * hg_w)
        w_hq = wb(o_hq, hg_w)
        w_xbc = wb(o_xbc, xbc_w)
        w_dt = jnp.concatenate([wb(o_dtf, 2 * SSM_HEADS), jnp.zeros((d, LANES - 2 * SSM_HEADS), BF16)], axis=1)
        w_gz = jnp.concatenate([wb(o_z, d_inner), wb(o_hgate, hg_w), wb(o_gates, N_BRANCH * d)], axis=1)

        lb_f = lb[0, li].reshape(1, hg_w)
        lb_b = lb[1, li].reshape(1, hg_w)
        bias_g = ssm_dt_bias[li].reshape(2, SSM_GROUPS, hgrp).transpose(1, 0, 2)
        alog_g = ssm_a_log[li].astype(F32).reshape(2, SSM_GROUPS, hgrp).transpose(1, 0, 2)
        dskip = jnp.repeat(ssm_d[li], SSM_HEADDIM).reshape(SSM_GROUPS, 1, hgrp * SSM_HEADDIM)
        wba = w_branch_att[li].astype(BF16)
        wbh = w_branch_hg[li].astype(BF16)
        wbs = w_branch_ssm[li].astype(BF16)
        wo = w_out[li].astype(BF16)
        w1 = ffn_w1[li].astype(BF16)
        w3 = ffn_w3[li].astype(BF16)
        w2 = ffn_w2[li].astype(BF16)

        hc = _adaln_call(xc, norm1_g[li], shc_m, scc_m)
        kc = _proj_call(hc, w_k, BF16, name="proj_kc")
        vvc = _proj_call(hc, w_vv, BF16, name="proj_vvc")
        frc = _proj_call(hc, w_f, F32, name="proj_fc")
        xbcc = _conv_call(_proj_call(hc, w_xbc, BF16, name="proj_xbcc"), ssm_conv_w[li], ssm_conv_b[li])
        dtc = _proj_call(hc, w_dt, F32, name="proj_dtc")[:, :, :2 * SSM_HEADS]
        dtc_col, dtc_row = _dt_layouts(dtc, bn, lc)
        zero_g = jnp.zeros((bn, HG_HEADS, 2, LANES, LANES), F32)
        zero_s = jnp.zeros((bn, SSM_GROUPS, 2, SSM_STATE, hgrp * SSM_HEADDIM), F32)
        if last:
            _, sg_ctx = _gla_call(None, frc, vvc, ATT_HEADS, lb_f, lb_b, zero_g, with_out=False)
            _, ss_ctx = _ssd_call(xbcc, dtc_col, dtc_row, bias_g, alog_g, dskip, zero_s, with_out=False)
        else:
            qc = _proj_call(hc, w_q, BF16, epi="scale", scale=q_scale, name="proj_qc")
            hqc = _proj_call(hc, w_hq, BF16, epi="silu", name="proj_hqc")
            gzc = _proj_call(hc, w_gz, BF16, name="proj_gzc")
            oc_hg, sg_ctx = _gla_call(hqc, frc, vvc, ATT_HEADS, lb_f, lb_b, zero_g)
            yc, ss_ctx = _ssd_call(xbcc, dtc_col, dtc_row, bias_g, alog_g, dskip, zero_s)
            oc_att = _attn_call(lam, qc, [(kc, vvc, 0)], att_norm_g[li], 1.0 - lam_init)

        h = _adaln_call(x, norm1_g[li], sh_m, sc_m)
        q = _proj_call(h, w_q, BF16, epi="rope", tables=(cos_t * q_scale, sin_t * q_scale), name="proj_q")
        k = _proj_call(h, w_k, BF16, epi="rope", tables=(cos_t, sin_t), name="proj_k")
        vv = _proj_call(h, w_vv, BF16, name="proj_vv")
        fr = _proj_call(h, w_f, F32, name="proj_f")
        hq = _proj_call(h, w_hq, BF16, epi="silu", name="proj_hq")
        xbc = _conv_call(_proj_call(h, w_xbc, BF16, name="proj_xbc"), ssm_conv_w[li], ssm_conv_b[li])
        dt = _proj_call(h, w_dt, F32, name="proj_dt")[:, :, :2 * SSM_HEADS]
        gz = _proj_call(h, w_gz, BF16, name="proj_gz")
        dt_col, dt_row = _dt_layouts(dt, bn, l)

        o_att = _attn_call(lam, q, [(kc, vvc, 0), (k, vv, 0)], att_norm_g[li], 1.0 - lam_init)
        o_hg, _ = _gla_call(hq, fr, vv, ATT_HEADS, lb_f, lb_b, sg_ctx)
        y, _ = _ssd_call(xbc, dt_col, dt_row, bias_g, alog_g, dskip, ss_ctx)

        x = _merge_call(x, o_att, o_hg, y, gz, hg_norm_g[li], ssm_norm_g[li], g_m, wba, wbh, wbs, wo)
        x = _ffn_call(x, norm2_g[li], sh_f, sc_f, g_f, w1, w3, w2, final_g, final=last)

        if not last:
            xc = _merge_call(xc, oc_att, oc_hg, yc, gzc, hg_norm_g[li], ssm_norm_g[li], gc_m,
                             wba, wbh, wbs, wo)
            xc = _ffn_call(xc, norm2_g[li], shc_f, scc_f, gc_f, w1, w3, w2, final_g, final=False)
    return x
```

```python
import functools
import math

import jax
import jax.numpy as jnp
from jax import lax
from jax.experimental import pallas as pl
from jax.experimental.pallas import tpu as pltpu

F32 = jnp.float32
BF16 = jnp.bfloat16

EPS = 1e-6
GRID_W = 64
ROPE_THETA = 10000.0
ATT_HEADS = 8
ATT_HD = 64
HG_HEADS = 8
SSM_HEADS = 32
SSM_HEADDIM = 64
SSM_GROUPS = 4
SSM_STATE = 128
SSM_CONV = 5
N_BRANCH = 3

LANES = 128
SUBLANES = 8
VMEM_LIMIT = 56 * 1024 * 1024

CHUNK = 128
DIAG = 8
LEVELS = (64, 32, 16, 8)


def _cparams(*sem):
    return pltpu.CompilerParams(dimension_semantics=sem, vmem_limit_bytes=VMEM_LIMIT)


def _sigmoid(x):
    return 1.0 / (1.0 + jnp.exp(-x))


def _silu(x):
    return x * _sigmoid(x)


def _split3(x):
    hi = x.astype(BF16)
    r = x - hi.astype(F32)
    mid = r.astype(BF16)
    lo = (r - mid.astype(F32)).astype(BF16)
    return hi, mid, lo


def _dot(a, b):
    return jnp.dot(a, b, preferred_element_type=F32)


def _dot_nt(a, b):
    return lax.dot_general(a, b, (((1,), (1,)), ((), ())), preferred_element_type=F32)


def _dot_tn(a, b):
    return lax.dot_general(a, b, (((0,), (0,)), ((), ())), preferred_element_type=F32)


def _mod_kernel(c_ref, w_ref, b_ref, o_ref):
    a = _silu(c_ref[...])
    a_hi = a.astype(BF16)
    a_lo = (a - a_hi.astype(F32)).astype(BF16)
    w = w_ref[...]
    w_hi = w.astype(BF16)
    w_lo = (w - w_hi.astype(F32)).astype(BF16)
    o_ref[...] = _dot(a_hi, w_hi) + _dot(a_lo, w_hi) + _dot(a_hi, w_lo) + b_ref[...]


def _mod_call(c_rows, w_ada, b_ada):
    depth, d, n = w_ada.shape
    r = c_rows.shape[0]
    tn = 1536
    return pl.pallas_call(
        _mod_kernel,
        grid=(depth, n // tn),
        in_specs=[
            pl.BlockSpec((r, d), lambda l, j: (0, 0)),
            pl.BlockSpec((None, d, tn), lambda l, j: (l, 0, j)),
            pl.BlockSpec((None, 1, tn), lambda l, j: (l, 0, j)),
        ],
        out_specs=pl.BlockSpec((None, r, tn), lambda l, j: (l, 0, j)),
        out_shape=jax.ShapeDtypeStruct((depth, r, n), F32),
        compiler_params=_cparams("parallel", "parallel"),
        name="mod",
    )(c_rows, w_ada, b_ada.reshape(depth, 1, n))


def _adaln_kernel(x_ref, g_ref, sh_ref, sc_ref, o_ref):
    x = x_ref[...]
    y = x * lax.rsqrt(jnp.mean(x * x, axis=-1, keepdims=True) + EPS) * g_ref[...]
    o_ref[...] = (y * (1.0 + sc_ref[...]) + sh_ref[...]).astype(o_ref.dtype)


def _adaln_call(x, g, shift, scale):
    bn, l, d = x.shape
    tl = min(l, 512)
    vec = pl.BlockSpec((None, 1, d), lambda b, i: (b, 0, 0))
    return pl.pallas_call(
        _adaln_kernel,
        grid=(bn, l // tl),
        in_specs=[
            pl.BlockSpec((None, tl, d), lambda b, i: (b, i, 0)),
            pl.BlockSpec((1, d), lambda b, i: (0, 0)),
            vec, vec,
        ],
        out_specs=pl.BlockSpec((None, tl, d), lambda b, i: (b, i, 0)),
        out_shape=jax.ShapeDtypeStruct((bn, l, d), BF16),
        compiler_params=_cparams("parallel", "parallel"),
        name="adaln",
    )(x, g.reshape(1, d), shift.reshape(bn, 1, d), scale.reshape(bn, 1, d))


def _proj_kernel(*refs, epi, scale):
    if epi == "rope":
        h_ref, w_ref, cos_ref, sin_ref, o_ref = refs
    else:
        h_ref, w_ref, o_ref = refs
    acc = _dot(h_ref[...], w_ref[...])
    if epi == "rope":
        cos = cos_ref[...]
        sin = sin_ref[...]
        for s in range(acc.shape[1] // LANES):
            t = acc[:, s * LANES:(s + 1) * LANES]
            r = t * cos + pltpu.roll(t, LANES // 2, 1) * sin
            o_ref[:, s * LANES:(s + 1) * LANES] = r.astype(o_ref.dtype)
    elif epi == "silu":
        o_ref[...] = _silu(acc).astype(o_ref.dtype)
    elif epi == "scale":
        o_ref[...] = (acc * scale).astype(o_ref.dtype)
    else:
        o_ref[...] = acc.astype(o_ref.dtype)


def _projt_kernel(h_ref, wt_ref, o_ref):
    o_ref[...] = _dot_nt(wt_ref[...], h_ref[...]).astype(o_ref.dtype)


def _projt_call(h, wt, out_dtype, name="projt"):
    bn, l, k = h.shape
    n = wt.shape[0]
    tm = min(l, 1024)
    tn = min(n, 1024)
    return pl.pallas_call(
        _projt_kernel,
        grid=(bn, l // tm, n // tn),
        in_specs=[
            pl.BlockSpec((None, tm, k), lambda b, i, j: (b, i, 0)),
            pl.BlockSpec((tn, k), lambda b, i, j: (j, 0)),
        ],
        out_specs=pl.BlockSpec((None, tn, tm), lambda b, i, j: (b, j, i)),
        out_shape=jax.ShapeDtypeStruct((bn, n, l), out_dtype),
        compiler_params=_cparams("parallel", "parallel", "parallel"),
        name=name,
    )(h, wt)


def _proj_call(h, w, out_dtype, epi="none", tables=None, scale=1.0, name="proj"):
    bn, l, k = h.shape
    n = w.shape[1]
    tm = min(l, 1024)
    tn = min(n, 1024)
    nl = l // tm
    in_specs = [
        pl.BlockSpec((None, tm, k), lambda b, i, j: (b, i, 0)),
        pl.BlockSpec((k, tn), lambda b, i, j: (0, j)),
    ]
    args = [h, w]
    if epi == "rope":
        tab = pl.BlockSpec((tm, LANES), lambda b, i, j: (i, 0))
        in_specs += [tab, tab]
        args += list(tables)
    return pl.pallas_call(
        functools.partial(_proj_kernel, epi=epi, scale=scale),
        grid=(bn, nl, n // tn),
        in_specs=in_specs,
        out_specs=pl.BlockSpec((None, tm, tn), lambda b, i, j: (b, i, j)),
        out_shape=jax.ShapeDtypeStruct((bn, l, n), out_dtype),
        compiler_params=_cparams("parallel", "parallel", "parallel"),
        name=name,
    )(*args)


def _conv_kernel(u_ref, w_ref, b_ref, o_ref, pad_ref, *, rows):
    l, cb = u_ref.shape
    halo = SUBLANES
    pad_ref[0:halo, :] = jnp.zeros((halo, cb), F32)
    pad_ref[halo + l:2 * halo + l, :] = jnp.zeros((halo, cb), F32)
    for r0 in range(0, l, rows):
        pad_ref[halo + r0:halo + r0 + rows, :] = u_ref[r0:r0 + rows, :].astype(F32)
    w = w_ref[...]
    bias = b_ref[...]
    half = SSM_CONV // 2
    for r0 in range(0, l, rows):
        acc = bias + w[0:1, :] * pad_ref[halo + r0 - half:halo + r0 - half + rows, :]
        for k in range(1, SSM_CONV):
            s = halo + r0 + k - half
            acc = acc + w[k:k + 1, :] * pad_ref[s:s + rows, :]
        o_ref[r0:r0 + rows, :] = _silu(acc).astype(o_ref.dtype)


def _conv_call(u, w, b):
    bn, l, c = u.shape
    cb = 512
    rows = min(l, 512)
    return pl.pallas_call(
        functools.partial(_conv_kernel, rows=rows),
        grid=(bn, c // cb),
        in_specs=[
            pl.BlockSpec((None, l, cb), lambda b_, j: (b_, 0, j)),
            pl.BlockSpec((SSM_CONV, cb), lambda b_, j: (0, j)),
            pl.BlockSpec((1, cb), lambda b_, j: (0, j)),
        ],
        out_specs=pl.BlockSpec((None, l, cb), lambda b_, j: (b_, 0, j)),
        out_shape=jax.ShapeDtypeStruct((bn, l, c), BF16),
        scratch_shapes=[pltpu.VMEM((l + 2 * SUBLANES, cb), F32)],
        compiler_params=_cparams("parallel", "parallel"),
        name="conv",
    )(u, w, b.reshape(1, c))


ATT_TQ = 256
ATT_BLOCK_Q = 1024


def _attn_kernel(*refs, nsrc, post_scale):
    lam_ref, q_ref = refs[0], refs[1]
    k_refs = [refs[2 + 2 * i] for i in range(nsrc)]
    vt_refs = [refs[3 + 2 * i] for i in range(nsrc)]
    g_ref, o_ref = refs[2 + 2 * nsrc:4 + 2 * nsrc]
    s_scr = refs[4 + 2 * nsrc:]
    tq = ATT_TQ if q_ref.shape[0] % ATT_TQ == 0 else q_ref.shape[0]
    nt = q_ref.shape[0] // tq
    lane = lax.broadcasted_iota(jnp.int32, (tq, LANES), 1)
    second_map = ((lane // 32) % 2) == 1
    zero = jnp.zeros((tq, LANES), BF16)
    gain = g_ref[...] * post_scale
    lam = lam_ref[0]

    def key_chunks(sizes):
        out, off = [], 0
        for i in range(nsrc):
            lk = k_refs[i].shape[0]
            kc = next(s for s in sizes if lk % s == 0)
            out += [(i, c0, kc, off + c0) for c0 in range(0, lk, kc)]
            off += lk
        return out

    chunks = key_chunks((256, LANES))

    def stacked_q(t):
        q = q_ref[t * tq:(t + 1) * tq, :]
        return jnp.concatenate([jnp.where(second_map, zero, q), jnp.where(second_map, q, zero)], axis=0)

    def scores(qq, slot, chunk, mx):
        i, c0, kc, off = chunk
        s = _dot_nt(k_refs[i][c0:c0 + kc, :], qq)
        s_scr[slot][off:off + kc, :] = s
        cm = jnp.max(s.reshape(kc // SUBLANES, SUBLANES, 2 * tq), axis=0)
        return cm if mx is None else jnp.maximum(mx, cm)

    def probs_pv(slot, chunk, m, den, acc):
        i, c0, kc, off = chunk
        p = jnp.exp2(s_scr[slot][off:off + kc, :] - m)
        d = jnp.sum(p.reshape(kc // SUBLANES, SUBLANES, 2 * tq), axis=0)
        part = _dot(vt_refs[i][:, c0:c0 + kc], p.astype(BF16))
        return (d, part) if den is None else (den + d, acc + part)

    qq = stacked_q(0)
    mx = None
    for chunk in chunks:
        mx = scores(qq, 0, chunk, mx)
    for t in range(nt):
        slot = t % 2
        m = jnp.max(mx, axis=0, keepdims=True)
        mx, den, acc = None, None, None
        if t + 1 < nt:
            qq = stacked_q(t + 1)
        for chunk in chunks:
            if t + 1 < nt:
                mx = scores(qq, 1 - slot, chunk, mx)
            den, acc = probs_pv(slot, chunk, m, den, acc)
        den = jnp.sum(den, axis=0, keepdims=True)
        c0 = 1.0 / den[:, :tq]
        c1 = lam / den[:, tq:]
        o = (acc[:, :tq] * c0 - acc[:, tq:] * c1).T
        o = o * lax.rsqrt(jnp.mean(o * o, axis=1, keepdims=True) + EPS) * gain
        o_ref[t * tq:(t + 1) * tq, :] = o.astype(o_ref.dtype)


def _attn_call(lam, q, kvs, g, post_scale):
    bn, lq, hw = q.shape
    nh = hw // LANES
    tq = min(lq, ATT_BLOCK_Q)
    in_specs = [
        pl.BlockSpec(memory_space=pltpu.SMEM),
        pl.BlockSpec((None, tq, LANES), lambda b, h, i: (b, i, h)),
    ]
    args = [lam, q]
    for k, vt in kvs:
        lk = k.shape[1]
        in_specs += [pl.BlockSpec((None, lk, LANES), lambda b, h, i: (b, 0, h)),
                     pl.BlockSpec((None, LANES, lk), lambda b, h, i: (b, h, 0))]
        args += [k, vt]
    in_specs.append(pl.BlockSpec((1, LANES), lambda b, h, i: (0, 0)))
    args.append(g.reshape(1, LANES))
    lk_total = sum(k.shape[1] for k, _ in kvs)
    sub = ATT_TQ if tq % ATT_TQ == 0 else tq
    return pl.pallas_call(
        functools.partial(_attn_kernel, nsrc=len(kvs), post_scale=post_scale),
        grid=(bn, nh, lq // tq),
        in_specs=in_specs,
        out_specs=pl.BlockSpec((None, tq, LANES), lambda b, h, i: (b, i, h)),
        out_shape=jax.ShapeDtypeStruct((bn, lq, hw), BF16),
        scratch_shapes=[pltpu.VMEM((lk_total, 2 * sub), F32), pltpu.VMEM((lk_total, 2 * sub), F32)],
        compiler_params=_cparams("parallel", "parallel", "parallel"),
        name="diff_attn",
    )(*args)


N_GLA_MASKS = len(LEVELS) + 1


def _gla_fill_masks(tri_ref, mask_ref):
    ti = lax.broadcasted_iota(jnp.int32, (CHUNK, CHUNK), 0)
    si = lax.broadcasted_iota(jnp.int32, (CHUNK, CHUNK), 1)
    x = ti ^ si
    for d in (0, 1):
        strict = (ti < si) if d else (ti > si)
        incl = (ti <= si) if d else (ti >= si)
        tri_ref[d] = jnp.where(incl, 1.0, 0.0).astype(BF16)
        for li, h in enumerate(LEVELS):
            mask_ref[d * N_GLA_MASKS + li] = jnp.where(strict & (x >= h) & (x < 2 * h), 1.0, 0.0)
        mask_ref[d * N_GLA_MASKS + len(LEVELS)] = jnp.where(incl & (x < DIAG), 1.0, 0.0)


def _gla_chunk(q, fr, v, lb, st_ref, b_ref, tri_ref, mask_ref, rev, with_out):
    c = CHUNK
    mbase = N_GLA_MASKS if rev else 0
    tri = tri_ref[1 if rev else 0]
    e = jnp.exp(-jnp.abs(fr))
    r = 1.0 / (1.0 + e)
    pos = fr >= 0.0
    sig = jnp.where(pos, r, e * r)
    nsig = jnp.where(pos, e * r, r)
    lf = jnp.log2(lb + (1.0 - lb) * sig)
    k = (1.0 - lb) * nsig
    hi, mid, lo = _split3(lf)
    b = _dot(tri, hi) + _dot(tri, mid) + _dot(tri, lo)
    b_ref[...] = b
    b_tot = b_ref[0:1, :] if rev else b_ref[c - 1:c, :]
    st = st_ref[...]
    out = None
    if with_out:
        qf = q.astype(F32)
        row = lax.broadcasted_iota(jnp.int32, (c, LANES), 0)
        a = None
        for li, h in enumerate(LEVELS):
            pieces = []
            for base in range(0, c, 2 * h):
                rr = base + h if rev else base + h - 1
                pieces.append(-jnp.abs(b_ref[base:base + 2 * h, :] - b_ref[rr:rr + 1, :]))
            ex = jnp.exp2(jnp.concatenate(pieces, axis=0) if len(pieces) > 1 else pieces[0])
            late = ((row & h) == 0) if rev else ((row & h) != 0)
            z = (jnp.where(late, qf, k) * ex).astype(BF16)
            term = mask_ref[mbase + li] * _dot_nt(z, z)
            a = term if a is None else a + term
        pieces = []
        for base in range(0, c, DIAG):
            rr = base + DIAG - 1 if rev else base
            pieces.append(jnp.abs(b_ref[base:base + DIAG, :] - b_ref[rr:rr + 1, :]))
        d = jnp.concatenate(pieces, axis=0)
        qd = (qf * jnp.exp2(-d)).astype(BF16)
        kd = (k * jnp.exp2(d)).astype(BF16)
        a = a + jnp.where(mask_ref[mbase + len(LEVELS)] > 0.5, _dot_nt(qd, kd), 0.0)
        out = _dot(a.astype(BF16), v) + _dot_nt((qf * jnp.exp2(b)).astype(BF16), st.astype(BF16))
    kw = (k * jnp.exp2(b_tot - b)).astype(BF16)
    st_ref[...] = st * jnp.exp2(b_tot) + _dot_tn(v, kw)
    return out


GLA_UNROLL = 2


def _gla_kernel(*refs, nc, with_out):
    if with_out:
        (q_ref, frf_ref, frb_ref, v_ref, lbf_ref, lbb_ref, s0_ref, o_ref, st_out_ref,
         st_ref, b_ref, tri_ref, mask_ref) = refs
    else:
        (frf_ref, frb_ref, v_ref, lbf_ref, lbb_ref, s0_ref, st_out_ref,
         st_ref, b_ref, tri_ref, mask_ref) = refs
        q_ref = o_ref = None
    _gla_fill_masks(tri_ref, mask_ref)
    st_ref[...] = s0_ref[...]
    if with_out:
        o_ref[...] = jnp.zeros(o_ref.shape, o_ref.dtype)
    unroll = GLA_UNROLL if nc % GLA_UNROLL == 0 else 1

    def body(ci, carry):
        for u in range(unroll):
            for d, (fr_ref, lb_ref) in enumerate(((frf_ref, lbf_ref), (frb_ref, lbb_ref))):
                step = ci * unroll + u
                cc = step if d == 0 else nc - 1 - step
                rows = pl.ds(pl.multiple_of(cc * CHUNK, CHUNK), CHUNK)
                q = q_ref[rows, :] if with_out else None
                out = _gla_chunk(q, fr_ref[rows, :], v_ref[rows, :], lb_ref[...], st_ref.at[d],
                                 b_ref.at[2 * u + d], tri_ref, mask_ref, d == 1, with_out)
                if with_out:
                    o_ref[rows, :] += out
        return carry

    lax.fori_loop(0, nc // unroll, body, 0)
    st_out_ref[...] = st_ref[...]


def _gla_call(q, fr, v, voff, lb_f, lb_b, s0, with_out=True):
    bn, l, _ = v.shape
    nh = HG_HEADS
    hw = nh * LANES
    seq = lambda off: pl.BlockSpec((None, l, LANES), lambda b, h: (b, 0, h + off))
    lbs = pl.BlockSpec((1, LANES), lambda b, h: (0, h))
    sts = pl.BlockSpec((None, None, 2, LANES, LANES), lambda b, h: (b, h, 0, 0, 0))
    in_specs = [seq(0), seq(nh), seq(voff), lbs, lbs, sts]
    args = [fr, fr, v, lb_f, lb_b, s0]
    out_specs = [sts]
    out_shape = [jax.ShapeDtypeStruct(s0.shape, F32)]
    if with_out:
        in_specs = [seq(0)] + in_specs
        args = [q] + args
        out_specs = [seq(0)] + out_specs
        out_shape = [jax.ShapeDtypeStruct((bn, l, hw), F32)] + out_shape
    res = pl.pallas_call(
        functools.partial(_gla_kernel, nc=l // CHUNK, with_out=with_out),
        grid=(bn, nh),
        in_specs=in_specs,
        out_specs=out_specs,
        out_shape=out_shape,
        scratch_shapes=[pltpu.VMEM((2, LANES, LANES), F32), pltpu.VMEM((2 * GLA_UNROLL, CHUNK, LANES), F32),
                        pltpu.VMEM((2, CHUNK, CHUNK), BF16), pltpu.VMEM((2 * N_GLA_MASKS, CHUNK, CHUNK), F32)],
        compiler_params=_cparams("parallel", "parallel"),
        name="hgrn2",
    )(*args)
    return (res[0], res[1]) if with_out else (None, res[0])


def _softplus(x):
    return jnp.maximum(x, 0.0) + jnp.log1p(jnp.exp(-jnp.abs(x)))


def _ssd_chunk(xs, bc, cc, cb, dt_col_raw, dt_row_raw, bias_row, bias_col, alog_row, alog_col,
               sn_ref, rev, with_out):
    t = CHUNK
    hp = xs.shape[1]
    npair = hp // LANES
    ti = lax.broadcasted_iota(jnp.int32, (t, t), 0)
    si = lax.broadcasted_iota(jnp.int32, (t, t), 1)
    incl = (ti <= si) if rev else (ti >= si)
    tri = jnp.where(incl, 1.0, 0.0).astype(BF16)
    lane = lax.broadcasted_iota(jnp.int32, (t, LANES), 1)
    first = lane < SSM_HEADDIM

    dt_col = _softplus(dt_col_raw + bias_row)
    a_col = dt_col * (-jnp.exp(alog_row))
    hi, mid, lo = _split3(a_col)
    acum_col = _dot(tri, hi) + _dot(tri, mid) + _dot(tri, lo)
    if with_out:
        dt_row = _softplus(dt_row_raw + bias_col)
        a_row = dt_row * (-jnp.exp(alog_col))
        hi, mid, lo = _split3(a_row)
        acum_row = _dot_nt(hi, tri) + _dot_nt(mid, tri) + _dot_nt(lo, tri)

    def pair_cols(m, i):
        return jnp.where(first, m[:, 2 * i:2 * i + 1], m[:, 2 * i + 1:2 * i + 2])

    sn = sn_ref[...]
    ys, xws, dec = [], [], []
    for i in range(npair):
        dt_p = pair_cols(dt_col, i)
        ac_p = pair_cols(acum_col, i)
        xdt = xs[:, i * LANES:(i + 1) * LANES].astype(F32) * dt_p
        atot = ac_p[0:1, :] if rev else ac_p[t - 1:t, :]
        xws.append((xdt * jnp.exp(atot - ac_p)).astype(BF16))
        dec.append(jnp.exp(atot))
        if with_out:
            ws = []
            for j in (2 * i, 2 * i + 1):
                diff = acum_col[:, j:j + 1] - acum_row[j:j + 1, :]
                lm = jnp.where(incl, jnp.exp(jnp.where(incl, diff, 0.0)), 0.0)
                ws.append((cb * lm).astype(BF16))
            lhs = jnp.concatenate(ws, axis=1)
            zero = jnp.zeros_like(xdt)
            rhs = jnp.concatenate([jnp.where(first, xdt, zero), jnp.where(first, zero, xdt)],
                                  axis=0).astype(BF16)
            y_state = _dot(cc, sn[:, i * LANES:(i + 1) * LANES].astype(BF16)) * jnp.exp(ac_p)
            ys.append(_dot(lhs, rhs) + y_state)
    xw = jnp.concatenate(xws, axis=1)
    sn_ref[...] = sn * jnp.concatenate(dec, axis=1) + _dot_tn(bc, xw)
    return jnp.concatenate(ys, axis=1) if with_out else None


def _ssd_kernel(*refs, nc, with_out):
    if with_out:
        (xs_ref, b_ref, c_ref, dtc_ref, dtr_ref, biasr_ref, biasc_ref, alogr_ref, alogc_ref,
         dskip_ref, s0_ref, y_ref, st_out_ref, sn_ref) = refs
    else:
        (xs_ref, b_ref, dtc_ref, biasr_ref, alogr_ref, s0_ref, st_out_ref, sn_ref) = refs
        c_ref = dtr_ref = biasc_ref = alogc_ref = dskip_ref = y_ref = None
    sn_ref[...] = s0_ref[...]
    if with_out:
        dsk = dskip_ref[...]

        def init(ci, carry):
            rows = pl.ds(pl.multiple_of(ci * CHUNK, CHUNK), CHUNK)
            y_ref[rows, :] = xs_ref[rows, :].astype(F32) * dsk
            return carry

        lax.fori_loop(0, nc, init, 0)

    def body(ci, carry):
        for d in (0, 1):
            cidx = ci if d == 0 else nc - 1 - ci
            start = pl.multiple_of(cidx * CHUNK, CHUNK)
            rows = pl.ds(start, CHUNK)
            xs = xs_ref[rows, :]
            bc = b_ref[rows, :]
            if with_out:
                cc = c_ref[rows, :]
                cb = _dot_nt(cc, bc)
                y = _ssd_chunk(xs, bc, cc, cb, dtc_ref[d, rows, :], dtr_ref[d, :, rows],
                               biasr_ref[d], biasc_ref[d], alogr_ref[d], alogc_ref[d],
                               sn_ref.at[d], d == 1, True)
                y_ref[rows, :] += y
            else:
                _ssd_chunk(xs, bc, None, None, dtc_ref[d, rows, :], None, biasr_ref[d], None,
                           alogr_ref[d], None, sn_ref.at[d], d == 1, False)
        return carry

    lax.fori_loop(0, nc, body, 0)
    st_out_ref[...] = sn_ref[...]


def _ssd_call(xbc, dt_col, dt_row, bias, alog, dskip, s0, with_out=True):
    bn, l, _ = xbc.shape
    g = SSM_GROUPS
    hg = SSM_HEADS // g
    hp = hg * SSM_HEADDIM
    d_inner = SSM_HEADS * SSM_HEADDIM
    nb_off = d_inner // LANES
    xs_spec = pl.BlockSpec((None, l, hp), lambda b, gi: (b, 0, gi))
    b_spec = pl.BlockSpec((None, l, LANES), lambda b, gi: (b, 0, nb_off + gi))
    c_spec = pl.BlockSpec((None, l, LANES), lambda b, gi: (b, 0, nb_off + g + gi))
    dtc_spec = pl.BlockSpec((None, None, 2, l, hg), lambda b, gi: (b, gi, 0, 0, 0))
    dtr_spec = pl.BlockSpec((None, None, 2, hg, l), lambda b, gi: (b, gi, 0, 0, 0))
    prow = pl.BlockSpec((None, 2, 1, hg), lambda b, gi: (gi, 0, 0, 0))
    pcol = pl.BlockSpec((None, 2, hg, 1), lambda b, gi: (gi, 0, 0, 0))
    dsk_spec = pl.BlockSpec((None, 1, hp), lambda b, gi: (gi, 0, 0))
    st_spec = pl.BlockSpec((None, None, 2, SSM_STATE, hp), lambda b, gi: (b, gi, 0, 0, 0))
    bias_r = bias.reshape(g, 2, 1, hg)
    bias_c = bias.reshape(g, 2, hg, 1)
    alog_r = alog.reshape(g, 2, 1, hg)
    alog_c = alog.reshape(g, 2, hg, 1)
    if with_out:
        in_specs = [xs_spec, b_spec, c_spec, dtc_spec, dtr_spec, prow, pcol, prow, pcol, dsk_spec, st_spec]
        args = [xbc, xbc, xbc, dt_col, dt_row, bias_r, bias_c, alog_r, alog_c, dskip, s0]
        out_specs = [xs_spec, st_spec]
        out_shape = [jax.ShapeDtypeStruct((bn, l, d_inner), F32), jax.ShapeDtypeStruct(s0.shape, F32)]
    else:
        in_specs = [xs_spec, b_spec, dtc_spec, prow, prow, st_spec]
        args = [xbc, xbc, dt_col, bias_r, alog_r, s0]
        out_specs = [st_spec]
        out_shape = [jax.ShapeDtypeStruct(s0.shape, F32)]
    res = pl.pallas_call(
        functools.partial(_ssd_kernel, nc=l // CHUNK, with_out=with_out),
        grid=(bn, g),
        in_specs=in_specs,
        out_specs=out_specs,
        out_shape=out_shape,
        scratch_shapes=[pltpu.VMEM((2, SSM_STATE, hp), F32)],
        compiler_params=_cparams("parallel", "parallel"),
        name="ssd",
    )(*args)
    return (res[0], res[1]) if with_out else (None, res[0])


def _merge_kernel(x_ref, oatt_ref, ohg_ref, hgate_ref, y_ref, z_ref, gates_ref, hgn_ref, ssn_ref,
                  gm_ref, wba_ref, wbh_ref, wbs_ref, wout_ref, o_ref):
    d = x_ref.shape[1]
    hgn = hgn_ref[...]
    parts = []
    for s in range(ohg_ref.shape[1] // LANES):
        sl = slice(s * LANES, (s + 1) * LANES)
        o = ohg_ref[:, sl]
        o = o * lax.rsqrt(jnp.mean(o * o, axis=1, keepdims=True) + EPS) * hgn
        parts.append((o * _silu(hgate_ref[:, sl].astype(F32))).astype(BF16))
    o_hg = jnp.concatenate(parts, axis=1)
    ys = y_ref[...] * _silu(z_ref[...].astype(F32))
    o_ssm = (ys * lax.rsqrt(jnp.mean(ys * ys, axis=1, keepdims=True) + EPS) * ssn_ref[...]).astype(BF16)
    g_att = _sigmoid(gates_ref[:, 0:d].astype(F32))
    g_hg = _sigmoid(gates_ref[:, d:2 * d].astype(F32))
    g_ssm = _sigmoid(gates_ref[:, 2 * d:3 * d].astype(F32))
    m = (g_att * _dot(oatt_ref[...], wba_ref[...]) + g_hg * _dot(o_hg, wbh_ref[...])
         + g_ssm * _dot(o_ssm, wbs_ref[...]))
    mix = _dot(m.astype(BF16), wout_ref[...])
    o_ref[...] = x_ref[...] + gm_ref[...] * mix


def _merge_call(x, o_att, o_hg, y, gz, hg_norm_g, ssm_norm_g, g_m, wba, wbh, wbs, wout):
    bn, l, d = x.shape
    tm = min(l, 256)
    di = y.shape[2]
    assert di == 2 * d and gz.shape[2] == 6 * d
    tok = lambda w, j=0: pl.BlockSpec((None, tm, w), lambda b, i: (b, i, j))
    full = lambda a: pl.BlockSpec(a.shape, lambda b, i: (0,) * a.ndim)
    hgn = hg_norm_g.reshape(1, LANES)
    ssn = ssm_norm_g.reshape(1, di)
    return pl.pallas_call(
        _merge_kernel,
        grid=(bn, l // tm),
        in_specs=[tok(d), tok(d), tok(d), tok(d, 2), tok(di), tok(di, 0), tok(N_BRANCH * d, 1), full(hgn), full(ssn),
                  pl.BlockSpec((None, 1, d), lambda b, i: (b, 0, 0)),
                  full(wba), full(wbh), full(wbs), full(wout)],
        out_specs=tok(d),
        out_shape=jax.ShapeDtypeStruct((bn, l, d), F32),
        compiler_params=_cparams("parallel", "parallel"),
        name="merge",
    )(x, o_att, o_hg, gz, y, gz, gz, hgn, ssn, g_m.reshape(bn, 1, d), wba, wbh, wbs, wout)


def _ffn_kernel(x_ref, g_ref, sh_ref, sc_ref, gf_ref, w1_ref, w3_ref, w2_ref, fin_ref, o_ref, *, final):
    x = x_ref[...]
    h = x * lax.rsqrt(jnp.mean(x * x, axis=-1, keepdims=True) + EPS) * g_ref[...]
    h = (h * (1.0 + sc_ref[...]) + sh_ref[...]).astype(BF16)
    u = _dot(h, w1_ref[...])
    t = _dot(h, w3_ref[...])
    a = (_silu(u) * t).astype(BF16)
    y = x + gf_ref[...] * _dot(a, w2_ref[...])
    if final:
        y = y * lax.rsqrt(jnp.mean(y * y, axis=-1, keepdims=True) + EPS) * fin_ref[...]
    o_ref[...] = y


def _ffn_call(x, g, shift, scale, gate, w1, w3, w2, final_g, final):
    bn, l, d = x.shape
    tm = min(l, 256)
    tok = pl.BlockSpec((None, tm, d), lambda b, i: (b, i, 0))
    vec = pl.BlockSpec((None, 1, d), lambda b, i: (b, 0, 0))
    row = pl.BlockSpec((1, d), lambda b, i: (0, 0))
    full = lambda a: pl.BlockSpec(a.shape, lambda b, i: (0, 0))
    return pl.pallas_call(
        functools.partial(_ffn_kernel, final=final),
        grid=(bn, l // tm),
        in_specs=[tok, row, vec, vec, vec, full(w1), full(w3), full(w2), row],
        out_specs=tok,
        out_shape=jax.ShapeDtypeStruct((bn, l, d), F32),
        compiler_params=_cparams("parallel", "parallel"),
        name="ffn",
    )(x, g.reshape(1, d), shift.reshape(bn, 1, d), scale.reshape(bn, 1, d), gate.reshape(bn, 1, d),
      w1, w3, w2, final_g.reshape(1, d))


def _rope_perm():
    idx = []
    for lane in range(LANES):
        half, mp, axis, i = lane // 64, (lane // 32) % 2, (lane // 16) % 2, lane % 16
        idx.append(mp * 64 + axis * 32 + half * 16 + i)
    return idx


def _rope_tables(l):
    rows = l // GRID_W
    row = jnp.repeat(jnp.arange(rows, dtype=F32), GRID_W)
    col = jnp.broadcast_to(jnp.arange(GRID_W, dtype=F32), (rows, GRID_W)).reshape(-1)
    nfreq = ATT_HD // 4
    freqs = ROPE_THETA ** (-jnp.arange(nfreq, dtype=F32) / nfreq)
    ang_r = row[:, None] * freqs[None, :]
    ang_c = col[:, None] * freqs[None, :]
    ang = jnp.concatenate([ang_r, ang_c, ang_r, ang_c], axis=1)
    ang = jnp.concatenate([ang, ang], axis=1)
    sign = jnp.where(jnp.arange(LANES) < LANES // 2, -1.0, 1.0).astype(F32)
    return jnp.cos(ang), jnp.sin(ang) * sign[None, :]


def _dt_layouts(dt_raw, bn, l):
    hg = SSM_HEADS // SSM_GROUPS
    t = dt_raw.reshape(bn, l, 2, SSM_GROUPS, hg)
    return t.transpose(0, 3, 2, 1, 4), t.transpose(0, 3, 2, 4, 1)


def kernel(x, c, ctx, c_ctx, w_ada, b_ada, norm1_g, w_in, att_lambda, att_norm_g, hg_lb_logits, hg_norm_g,
           ssm_conv_w, ssm_conv_b, ssm_dt_bias, ssm_a_log, ssm_d, ssm_norm_g, w_branch_att, w_branch_hg,
           w_branch_ssm, w_out, norm2_g, ffn_w1, ffn_w3, ffn_w2, final_g):
    bn, l, d = x.shape
    lc = ctx.shape[1]
    depth = w_ada.shape[0]
    att_w = ATT_HEADS * 2 * ATT_HD
    hg_w = HG_HEADS * LANES
    d_inner = SSM_HEADS * SSM_HEADDIM
    xbc_w = d_inner + 2 * SSM_GROUPS * SSM_STATE
    hgrp = SSM_HEADS // SSM_GROUPS

    sizes = (att_w, att_w, hg_w, hg_w, hg_w, xbc_w, SSM_HEADS, SSM_HEADS, att_w, hg_w, hg_w, d_inner, N_BRANCH * d)
    offs = [0]
    for s in sizes:
        offs.append(offs[-1] + s)
    (o_ak, o_av, o_hff, o_hfb, o_hi, o_xbc, o_dtf, o_dtb, o_aq, o_hq, o_hgate, o_z, o_gates, _) = offs

    perm = jnp.asarray([h * LANES + p for h in range(ATT_HEADS) for p in _rope_perm()], jnp.int32)
    cos_t, sin_t = _rope_tables(l)
    q_scale = ATT_HD ** -0.5 * math.log2(math.e)

    lb_w = jax.nn.softmax(hg_lb_logits.astype(F32), axis=1)
    lb = jnp.cumsum(lb_w, axis=1) - lb_w[:, :1]

    pad_rows = (-(bn + 1)) % SUBLANES
    c_rows = jnp.concatenate([c, c_ctx[None, :], jnp.zeros((pad_rows, d), F32)], axis=0)
    mod_all = _mod_call(c_rows, w_ada, b_ada)

    xc = ctx
    for li in range(depth):
        last = li == depth - 1
        mod = mod_all[li, :bn]
        mod_c = jnp.broadcast_to(mod_all[li, bn:bn + 1], (bn, 6 * d))
        sh_m, sc_m, g_m, sh_f, sc_f, g_f = [mod[:, i * d:(i + 1) * d] for i in range(6)]
        shc_m, scc_m, gc_m, shc_f, scc_f, gc_f = [mod_c[:, i * d:(i + 1) * d] for i in range(6)]

        lam_init = 0.8 - 0.6 * math.exp(-0.3 * li)
        al = att_lambda[li].astype(F32)
        lam = (jnp.exp(jnp.sum(al[0] * al[1])) - jnp.exp(jnp.sum(al[2] * al[3])) + lam_init).reshape(1)

        w = w_in[li]
        wb = lambda o, n: w[:, o:o + n].astype(BF16)
        w_q = wb(o_aq, att_w)[:, perm]
        w_k = wb(o_ak, att_w)[:, perm]
        w_vt = wb(o_av, att_w).T
        w_hi = wb(o_hi, hg_w)
        w_f = wb(o_hff, 2 * hg_w)
        w_hq = wb(o_hq, hg_w)
        w_xbc = wb(o_xbc, xbc_w)
        w_dt = jnp.concatenate([wb(o_dtf, 2 * SSM_HEADS), jnp.zeros((d, LANES - 2 * SSM_HEADS), BF16)], axis=1)
        w_gz = jnp.concatenate([wb(o_z, d_inner), wb(o_hgate, hg_w), wb(o_gates, N_BRANCH * d)], axis=1)

        lb_f = lb[0, li].reshape(1, hg_w)
        lb_b = lb[1, li].reshape(1, hg_w)
        bias_g = ssm_dt_bias[li].reshape(2, SSM_GROUPS, hgrp).transpose(1, 0, 2)
        alog_g = ssm_a_log[li].astype(F32).reshape(2, SSM_GROUPS, hgrp).transpose(1, 0, 2)
        dskip = jnp.repeat(ssm_d[li], SSM_HEADDIM).reshape(SSM_GROUPS, 1, hgrp * SSM_HEADDIM)
        wba = w_branch_att[li].astype(BF16)
        wbh = w_branch_hg[li].astype(BF16)
        wbs = w_branch_ssm[li].astype(BF16)
        wo = w_out[li].astype(BF16)
        w1 = ffn_w1[li].astype(BF16)
        w3 = ffn_w3[li].astype(BF16)
        w2 = ffn_w2[li].astype(BF16)

        hc = _adaln_call(xc, norm1_g[li], shc_m, scc_m)
        kc = _proj_call(hc, w_k, BF16, name="proj_kc")
        vtc = _projt_call(hc, w_vt, BF16, name="proj_vtc")
        vhc = _proj_call(hc, w_hi, BF16, name="proj_vhc")
        frc = _proj_call(hc, w_f, F32, name="proj_fc")
        xbcc = _conv_call(_proj_call(hc, w_xbc, BF16, name="proj_xbcc"), ssm_conv_w[li], ssm_conv_b[li])
        dtc = _proj_call(hc, w_dt, F32, name="proj_dtc")[:, :, :2 * SSM_HEADS]
        dtc_col, dtc_row = _dt_layouts(dtc, bn, lc)
        zero_g = jnp.zeros((bn, HG_HEADS, 2, LANES, LANES), F32)
        zero_s = jnp.zeros((bn, SSM_GROUPS, 2, SSM_STATE, hgrp * SSM_HEADDIM), F32)
        if last:
            _, sg_ctx = _gla_call(None, frc, vhc, 0, lb_f, lb_b, zero_g, with_out=False)
            _, ss_ctx = _ssd_call(xbcc, dtc_col, dtc_row, bias_g, alog_g, dskip, zero_s, with_out=False)
        else:
            qc = _proj_call(hc, w_q, BF16, epi="scale", scale=q_scale, name="proj_qc")
            hqc = _proj_call(hc, w_hq, BF16, epi="silu", name="proj_hqc")
            gzc = _proj_call(hc, w_gz, BF16, name="proj_gzc")
            oc_hg, sg_ctx = _gla_call(hqc, frc, vhc, 0, lb_f, lb_b, zero_g)
            yc, ss_ctx = _ssd_call(xbcc, dtc_col, dtc_row, bias_g, alog_g, dskip, zero_s)
            oc_att = _attn_call(lam, qc, [(kc, vtc)], att_norm_g[li], 1.0 - lam_init)

        h = _adaln_call(x, norm1_g[li], sh_m, sc_m)
        q = _proj_call(h, w_q, BF16, epi="rope", tables=(cos_t * q_scale, sin_t * q_scale), name="proj_q")
        k = _proj_call(h, w_k, BF16, epi="rope", tables=(cos_t, sin_t), name="proj_k")
        vt = _projt_call(h, w_vt, BF16, name="proj_vt")
        vh = _proj_call(h, w_hi, BF16, name="proj_vh")
        fr = _proj_call(h, w_f, F32, name="proj_f")
        hq = _proj_call(h, w_hq, BF16, epi="silu", name="proj_hq")
        xbc = _conv_call(_proj_call(h, w_xbc, BF16, name="proj_xbc"), ssm_conv_w[li], ssm_conv_b[li])
        dt = _proj_call(h, w_dt, F32, name="proj_dt")[:, :, :2 * SSM_HEADS]
        gz = _proj_call(h, w_gz, BF16, name="proj_gz")
        dt_col, dt_row = _dt_layouts(dt, bn, l)

        o_att = _attn_call(lam, q, [(kc, vtc), (k, vt)], att_norm_g[li], 1.0 - lam_init)
        o_hg, _ = _gla_call(hq, fr, vh, 0, lb_f, lb_b, sg_ctx)
        y, _ = _ssd_call(xbc, dt_col, dt_row, bias_g, alog_g, dskip, ss_ctx)

        x = _merge_call(x, o_att, o_hg, y, gz, hg_norm_g[li], ssm_norm_g[li], g_m, wba, wbh, wbs, wo)
        x = _ffn_call(x, norm2_g[li], sh_f, sc_f, g_f, w1, w3, w2, final_g, final=last)

        if not last:
            xc = _merge_call(xc, oc_att, oc_hg, yc, gzc, hg_norm_g[li], ssm_norm_g[li], gc_m,
                             wba, wbh, wbs, wo)
            xc = _ffn_call(xc, norm2_g[li], shc_f, scc_f, gc_f, w1, w3, w2, final_g, final=False)
    return x
```

```python
import functools
import math

import jax
import jax.numpy as jnp
from jax import lax
from jax.experimental import pallas as pl
from jax.experimental.pallas import tpu as pltpu

F32 = jnp.float32
BF16 = jnp.bfloat16

EPS = 1e-6
GRID_W = 64
ROPE_THETA = 10000.0
ATT_HEADS = 8
ATT_HD = 64
HG_HEADS = 8
SSM_HEADS = 32
SSM_HEADDIM = 64
SSM_GROUPS = 4
SSM_STATE = 128
SSM_CONV = 5
N_BRANCH = 3

LANES = 128
SUBLANES = 8
VMEM_LIMIT = 56 * 1024 * 1024

CHUNK = 128
DIAG = 8
LEVELS = (64, 32, 16, 8)


def _cparams(*sem):
    return pltpu.CompilerParams(dimension_semantics=sem, vmem_limit_bytes=VMEM_LIMIT)


def _sigmoid(x):
    return 1.0 / (1.0 + jnp.exp(-x))


def _silu(x):
    return x * _sigmoid(x)


def _split3(x):
    hi = x.astype(BF16)
    r = x - hi.astype(F32)
    mid = r.astype(BF16)
    lo = (r - mid.astype(F32)).astype(BF16)
    return hi, mid, lo


def _dot(a, b):
    return jnp.dot(a, b, preferred_element_type=F32)


def _dot_nt(a, b):
    return lax.dot_general(a, b, (((1,), (1,)), ((), ())), preferred_element_type=F32)


def _dot_tn(a, b):
    return lax.dot_general(a, b, (((0,), (0,)), ((), ())), preferred_element_type=F32)


def _mod_kernel(c_ref, w_ref, b_ref, o_ref):
    a = _silu(c_ref[...])
    a_hi = a.astype(BF16)
    a_lo = (a - a_hi.astype(F32)).astype(BF16)
    w = w_ref[...]
    w_hi = w.astype(BF16)
    w_lo = (w - w_hi.astype(F32)).astype(BF16)
    o_ref[...] = _dot(a_hi, w_hi) + _dot(a_lo, w_hi) + _dot(a_hi, w_lo) + b_ref[...]


def _mod_call(c_rows, w_ada, b_ada):
    depth, d, n = w_ada.shape
    r = c_rows.shape[0]
    tn = 1536
    return pl.pallas_call(
        _mod_kernel,
        grid=(depth, n // tn),
        in_specs=[
            pl.BlockSpec((r, d), lambda l, j: (0, 0)),
            pl.BlockSpec((None, d, tn), lambda l, j: (l, 0, j)),
            pl.BlockSpec((None, 1, tn), lambda l, j: (l, 0, j)),
        ],
        out_specs=pl.BlockSpec((None, r, tn), lambda l, j: (l, 0, j)),
        out_shape=jax.ShapeDtypeStruct((depth, r, n), F32),
        compiler_params=_cparams("parallel", "parallel"),
        name="mod",
    )(c_rows, w_ada, b_ada.reshape(depth, 1, n))


def _adaln_kernel(x_ref, g_ref, sh_ref, sc_ref, o_ref):
    x = x_ref[...]
    y = x * lax.rsqrt(jnp.mean(x * x, axis=-1, keepdims=True) + EPS) * g_ref[...]
    o_ref[...] = (y * (1.0 + sc_ref[...]) + sh_ref[...]).astype(o_ref.dtype)


def _adaln_call(x, g, shift, scale):
    bn, l, d = x.shape
    tl = min(l, 512)
    vec = pl.BlockSpec((None, 1, d), lambda b, i: (b, 0, 0))
    return pl.pallas_call(
        _adaln_kernel,
        grid=(bn, l // tl),
        in_specs=[
            pl.BlockSpec((None, tl, d), lambda b, i: (b, i, 0)),
            pl.BlockSpec((1, d), lambda b, i: (0, 0)),
            vec, vec,
        ],
        out_specs=pl.BlockSpec((None, tl, d), lambda b, i: (b, i, 0)),
        out_shape=jax.ShapeDtypeStruct((bn, l, d), BF16),
        compiler_params=_cparams("parallel", "parallel"),
        name="adaln",
    )(x, g.reshape(1, d), shift.reshape(bn, 1, d), scale.reshape(bn, 1, d))


def _proj_kernel(*refs, epi, scale):
    if epi == "rope":
        h_ref, w_ref, cos_ref, sin_ref, o_ref = refs
    else:
        h_ref, w_ref, o_ref = refs
    acc = _dot(h_ref[...], w_ref[...])
    if epi == "rope":
        cos = cos_ref[...]
        sin = sin_ref[...]
        for s in range(acc.shape[1] // LANES):
            t = acc[:, s * LANES:(s + 1) * LANES]
            r = t * cos + pltpu.roll(t, LANES // 2, 1) * sin
            o_ref[:, s * LANES:(s + 1) * LANES] = r.astype(o_ref.dtype)
    elif epi == "silu":
        o_ref[...] = _silu(acc).astype(o_ref.dtype)
    elif epi == "scale":
        o_ref[...] = (acc * scale).astype(o_ref.dtype)
    else:
        o_ref[...] = acc.astype(o_ref.dtype)


def _projt_kernel(h_ref, wt_ref, o_ref):
    o_ref[...] = _dot_nt(wt_ref[...], h_ref[...]).astype(o_ref.dtype)


def _projt_call(h, wt, out_dtype, name="projt"):
    bn, l, k = h.shape
    n = wt.shape[0]
    tm = min(l, 1024)
    tn = min(n, 1024)
    return pl.pallas_call(
        _projt_kernel,
        grid=(bn, l // tm, n // tn),
        in_specs=[
            pl.BlockSpec((None, tm, k), lambda b, i, j: (b, i, 0)),
            pl.BlockSpec((tn, k), lambda b, i, j: (j, 0)),
        ],
        out_specs=pl.BlockSpec((None, tn, tm), lambda b, i, j: (b, j, i)),
        out_shape=jax.ShapeDtypeStruct((bn, n, l), out_dtype),
        compiler_params=_cparams("parallel", "parallel", "parallel"),
        name=name,
    )(h, wt)


def _proj_call(h, w, out_dtype, epi="none", tables=None, scale=1.0, name="proj"):
    bn, l, k = h.shape
    n = w.shape[1]
    tm = min(l, 1024)
    tn = min(n, 1024)
    nl = l // tm
    in_specs = [
        pl.BlockSpec((None, tm, k), lambda b, i, j: (b, i, 0)),
        pl.BlockSpec((k, tn), lambda b, i, j: (0, j)),
    ]
    args = [h, w]
    if epi == "rope":
        tab = pl.BlockSpec((tm, LANES), lambda b, i, j: (i, 0))
        in_specs += [tab, tab]
        args += list(tables)
    return pl.pallas_call(
        functools.partial(_proj_kernel, epi=epi, scale=scale),
        grid=(bn, nl, n // tn),
        in_specs=in_specs,
        out_specs=pl.BlockSpec((None, tm, tn), lambda b, i, j: (b, i, j)),
        out_shape=jax.ShapeDtypeStruct((bn, l, n), out_dtype),
        compiler_params=_cparams("parallel", "parallel", "parallel"),
        name=name,
    )(*args)


def _conv_kernel(u_ref, w_ref, b_ref, o_ref, pad_ref, *, rows):
    l, cb = u_ref.shape
    halo = SUBLANES
    pad_ref[0:halo, :] = jnp.zeros((halo, cb), F32)
    pad_ref[halo + l:2 * halo + l, :] = jnp.zeros((halo, cb), F32)
    for r0 in range(0, l, rows):
        pad_ref[halo + r0:halo + r0 + rows, :] = u_ref[r0:r0 + rows, :].astype(F32)
    w = w_ref[...]
    bias = b_ref[...]
    half = SSM_CONV // 2
    for r0 in range(0, l, rows):
        acc = bias + w[0:1, :] * pad_ref[halo + r0 - half:halo + r0 - half + rows, :]
        for k in range(1, SSM_CONV):
            s = halo + r0 + k - half
            acc = acc + w[k:k + 1, :] * pad_ref[s:s + rows, :]
        o_ref[r0:r0 + rows, :] = _silu(acc).astype(o_ref.dtype)


def _conv_call(u, w, b):
    bn, l, c = u.shape
    cb = 512
    rows = min(l, 512)
    return pl.pallas_call(
        functools.partial(_conv_kernel, rows=rows),
        grid=(bn, c // cb),
        in_specs=[
            pl.BlockSpec((None, l, cb), lambda b_, j: (b_, 0, j)),
            pl.BlockSpec((SSM_CONV, cb), lambda b_, j: (0, j)),
            pl.BlockSpec((1, cb), lambda b_, j: (0, j)),
        ],
        out_specs=pl.BlockSpec((None, l, cb), lambda b_, j: (b_, 0, j)),
        out_shape=jax.ShapeDtypeStruct((bn, l, c), BF16),
        scratch_shapes=[pltpu.VMEM((l + 2 * SUBLANES, cb), F32)],
        compiler_params=_cparams("parallel", "parallel"),
        name="conv",
    )(u, w, b.reshape(1, c))


ATT_TQ = 256
ATT_BLOCK_Q = 1024


def _attn_kernel(*refs, nsrc, post_scale):
    lam_ref, q_ref = refs[0], refs[1]
    k_refs = [refs[2 + 2 * i] for i in range(nsrc)]
    vt_refs = [refs[3 + 2 * i] for i in range(nsrc)]
    g_ref, o_ref = refs[2 + 2 * nsrc:4 + 2 * nsrc]
    s_scr = refs[4 + 2 * nsrc:]
    tq = ATT_TQ if q_ref.shape[0] % ATT_TQ == 0 else q_ref.shape[0]
    nt = q_ref.shape[0] // tq
    lane = lax.broadcasted_iota(jnp.int32, (tq, LANES), 1)
    second_map = ((lane // 32) % 2) == 1
    zero = jnp.zeros((tq, LANES), BF16)
    gain = g_ref[...] * post_scale
    lam = lam_ref[0]

    def key_chunks(sizes):
        out, off = [], 0
        for i in range(nsrc):
            lk = k_refs[i].shape[0]
            kc = next(s for s in sizes if lk % s == 0)
            out += [(i, c0, kc, off + c0) for c0 in range(0, lk, kc)]
            off += lk
        return out

    chunks = key_chunks((256, LANES))

    def stacked_q(t):
        q = q_ref[t * tq:(t + 1) * tq, :]
        return jnp.concatenate([jnp.where(second_map, zero, q), jnp.where(second_map, q, zero)], axis=0)

    def scores(qq, slot, chunk, mx):
        i, c0, kc, off = chunk
        s = _dot_nt(k_refs[i][c0:c0 + kc, :], qq)
        s_scr[slot][off:off + kc, :] = s
        cm = jnp.max(s.reshape(kc // SUBLANES, SUBLANES, 2 * tq), axis=0)
        return cm if mx is None else jnp.maximum(mx, cm)

    ones_rows = 2 * SUBLANES

    def probs_pv(slot, chunk, m, acc):
        i, c0, kc, off = chunk
        p = jnp.exp2(s_scr[slot][off:off + kc, :] - m).astype(BF16)
        vt1 = jnp.concatenate([vt_refs[i][:, c0:c0 + kc], jnp.ones((ones_rows, kc), BF16)], axis=0)
        part = _dot(vt1, p)
        return part if acc is None else acc + part

    qq = stacked_q(0)
    mx = None
    for chunk in chunks:
        mx = scores(qq, 0, chunk, mx)
    for t in range(nt):
        slot = t % 2
        m = jnp.max(mx, axis=0, keepdims=True)
        mx, acc = None, None
        if t + 1 < nt:
            qq = stacked_q(t + 1)
            for chunk in chunks:
                mx = scores(qq, 1 - slot, chunk, mx)
        for chunk in chunks:
            acc = probs_pv(slot, chunk, m, acc)
        den = acc[LANES:LANES + 1, :]
        c0 = 1.0 / den[:, :tq]
        c1 = lam / den[:, tq:]
        o = (acc[:LANES, :tq] * c0 - acc[:LANES, tq:] * c1).T
        o = o * lax.rsqrt(jnp.mean(o * o, axis=1, keepdims=True) + EPS) * gain
        o_ref[t * tq:(t + 1) * tq, :] = o.astype(o_ref.dtype)


def _attn_call(lam, q, kvs, g, post_scale):
    bn, lq, hw = q.shape
    nh = hw // LANES
    tq = min(lq, ATT_BLOCK_Q)
    in_specs = [
        pl.BlockSpec(memory_space=pltpu.SMEM),
        pl.BlockSpec((None, tq, LANES), lambda b, h, i: (b, i, h)),
    ]
    args = [lam, q]
    for k, vt in kvs:
        lk = k.shape[1]
        in_specs += [pl.BlockSpec((None, lk, LANES), lambda b, h, i: (b, 0, h)),
                     pl.BlockSpec((None, LANES, lk), lambda b, h, i: (b, h, 0))]
        args += [k, vt]
    in_specs.append(pl.BlockSpec((1, LANES), lambda b, h, i: (0, 0)))
    args.append(g.reshape(1, LANES))
    lk_total = sum(k.shape[1] for k, _ in kvs)
    sub = ATT_TQ if tq % ATT_TQ == 0 else tq
    return pl.pallas_call(
        functools.partial(_attn_kernel, nsrc=len(kvs), post_scale=post_scale),
        grid=(bn, nh, lq // tq),
        in_specs=in_specs,
        out_specs=pl.BlockSpec((None, tq, LANES), lambda b, h, i: (b, i, h)),
        out_shape=jax.ShapeDtypeStruct((bn, lq, hw), BF16),
        scratch_shapes=[pltpu.VMEM((lk_total, 2 * sub), F32), pltpu.VMEM((lk_total, 2 * sub), F32)],
        compiler_params=_cparams("parallel", "parallel", "parallel"),
        name="diff_attn",
    )(*args)


N_GLA_MASKS = len(LEVELS) + 1


def _gla_fill_masks(tri_ref, mask_ref):
    ti = lax.broadcasted_iota(jnp.int32, (CHUNK, CHUNK), 0)
    si = lax.broadcasted_iota(jnp.int32, (CHUNK, CHUNK), 1)
    x = ti ^ si
    for d in (0, 1):
        strict = (ti < si) if d else (ti > si)
        incl = (ti <= si) if d else (ti >= si)
        tri_ref[d] = jnp.where(incl, 1.0, 0.0).astype(BF16)
        for li, h in enumerate(LEVELS):
            mask_ref[d * N_GLA_MASKS + li] = jnp.where(strict & (x >= h) & (x < 2 * h), 1.0, 0.0)
        mask_ref[d * N_GLA_MASKS + len(LEVELS)] = jnp.where(incl & (x < DIAG), 1.0, 0.0)


def _gla_chunk(q, fr, v, lb, st_ref, b_ref, tri_ref, mask_ref, rev, with_out):
    c = CHUNK
    mbase = N_GLA_MASKS if rev else 0
    tri = tri_ref[1 if rev else 0]
    e = jnp.exp(-jnp.abs(fr))
    r = 1.0 / (1.0 + e)
    pos = fr >= 0.0
    sig = jnp.where(pos, r, e * r)
    nsig = jnp.where(pos, e * r, r)
    lf = jnp.log2(lb + (1.0 - lb) * sig)
    k = (1.0 - lb) * nsig
    hi, mid, lo = _split3(lf)
    b = _dot(tri, hi) + _dot(tri, mid) + _dot(tri, lo)
    b_ref[...] = b
    b_tot = b_ref[0:1, :] if rev else b_ref[c - 1:c, :]
    st = st_ref[...]
    out = None
    if with_out:
        qf = q.astype(F32)
        row = lax.broadcasted_iota(jnp.int32, (c, LANES), 0)
        a = None
        for li, h in enumerate(LEVELS):
            pieces = []
            for base in range(0, c, 2 * h):
                rr = base + h if rev else base + h - 1
                pieces.append(-jnp.abs(b_ref[base:base + 2 * h, :] - b_ref[rr:rr + 1, :]))
            ex = jnp.exp2(jnp.concatenate(pieces, axis=0) if len(pieces) > 1 else pieces[0])
            late = ((row & h) == 0) if rev else ((row & h) != 0)
            z = (jnp.where(late, qf, k) * ex).astype(BF16)
            term = mask_ref[mbase + li] * _dot_nt(z, z)
            a = term if a is None else a + term
        pieces = []
        for base in range(0, c, DIAG):
            rr = base + DIAG - 1 if rev else base
            pieces.append(jnp.abs(b_ref[base:base + DIAG, :] - b_ref[rr:rr + 1, :]))
        d = jnp.concatenate(pieces, axis=0)
        qd = (qf * jnp.exp2(-d)).astype(BF16)
        kd = (k * jnp.exp2(d)).astype(BF16)
        a = a + jnp.where(mask_ref[mbase + len(LEVELS)] > 0.5, _dot_nt(qd, kd), 0.0)
        out = _dot(a.astype(BF16), v) + _dot_nt((qf * jnp.exp2(b)).astype(BF16), st.astype(BF16))
    kw = (k * jnp.exp2(b_tot - b)).astype(BF16)
    st_ref[...] = st * jnp.exp2(b_tot) + _dot_tn(v, kw)
    return out


GLA_UNROLL = 2


def _gla_kernel(*refs, nc, with_out):
    if with_out:
        (q_ref, frf_ref, frb_ref, v_ref, lbf_ref, lbb_ref, s0_ref, o_ref, st_out_ref,
         st_ref, b_ref, tri_ref, mask_ref) = refs
    else:
        (frf_ref, frb_ref, v_ref, lbf_ref, lbb_ref, s0_ref, st_out_ref,
         st_ref, b_ref, tri_ref, mask_ref) = refs
        q_ref = o_ref = None
    _gla_fill_masks(tri_ref, mask_ref)
    st_ref[...] = s0_ref[...]
    if with_out:
        o_ref[...] = jnp.zeros(o_ref.shape, o_ref.dtype)
    unroll = GLA_UNROLL if nc % GLA_UNROLL == 0 else 1

    def body(ci, carry):
        for u in range(unroll):
            for d, (fr_ref, lb_ref) in enumerate(((frf_ref, lbf_ref), (frb_ref, lbb_ref))):
                step = ci * unroll + u
                cc = step if d == 0 else nc - 1 - step
                rows = pl.ds(pl.multiple_of(cc * CHUNK, CHUNK), CHUNK)
                q = q_ref[rows, :] if with_out else None
                out = _gla_chunk(q, fr_ref[rows, :], v_ref[rows, :], lb_ref[...], st_ref.at[d],
                                 b_ref.at[2 * u + d], tri_ref, mask_ref, d == 1, with_out)
                if with_out:
                    o_ref[rows, :] += out
        return carry

    lax.fori_loop(0, nc // unroll, body, 0)
    st_out_ref[...] = st_ref[...]


def _gla_call(q, fr, v, voff, lb_f, lb_b, s0, with_out=True):
    bn, l, _ = v.shape
    nh = HG_HEADS
    hw = nh * LANES
    seq = lambda off: pl.BlockSpec((None, l, LANES), lambda b, h: (b, 0, h + off))
    lbs = pl.BlockSpec((1, LANES), lambda b, h: (0, h))
    sts = pl.BlockSpec((None, None, 2, LANES, LANES), lambda b, h: (b, h, 0, 0, 0))
    in_specs = [seq(0), seq(nh), seq(voff), lbs, lbs, sts]
    args = [fr, fr, v, lb_f, lb_b, s0]
    out_specs = [sts]
    out_shape = [jax.ShapeDtypeStruct(s0.shape, F32)]
    if with_out:
        in_specs = [seq(0)] + in_specs
        args = [q] + args
        out_specs = [seq(0)] + out_specs
        out_shape = [jax.ShapeDtypeStruct((bn, l, hw), F32)] + out_shape
    res = pl.pallas_call(
        functools.partial(_gla_kernel, nc=l // CHUNK, with_out=with_out),
        grid=(bn, nh),
        in_specs=in_specs,
        out_specs=out_specs,
        out_shape=out_shape,
        scratch_shapes=[pltpu.VMEM((2, LANES, LANES), F32), pltpu.VMEM((2 * GLA_UNROLL, CHUNK, LANES), F32),
                        pltpu.VMEM((2, CHUNK, CHUNK), BF16), pltpu.VMEM((2 * N_GLA_MASKS, CHUNK, CHUNK), F32)],
        compiler_params=_cparams("parallel", "parallel"),
        name="hgrn2",
    )(*args)
    return (res[0], res[1]) if with_out else (None, res[0])


def _softplus(x):
    return jnp.maximum(x, 0.0) + jnp.log1p(jnp.exp(-jnp.abs(x)))


LOG2E = math.log2(math.e)


def _ssd_fill_masks(tri_ref, mask_ref):
    ti = lax.broadcasted_iota(jnp.int32, (CHUNK, CHUNK), 0)
    si = lax.broadcasted_iota(jnp.int32, (CHUNK, CHUNK), 1)
    for d in (0, 1):
        incl = jnp.where((ti <= si) if d else (ti >= si), 1.0, 0.0)
        mask_ref[d] = incl
        tri_ref[d] = incl.astype(BF16)


def _ssd_chunk(xs, bc, cc, cb, dt_col_raw, dt_row_raw, bias_row, bias_col, alog_row, alog_col,
               sn_ref, tri, rev, with_out):
    t = CHUNK
    hp = xs.shape[1]
    npair = hp // LANES
    lane = lax.broadcasted_iota(jnp.int32, (t, LANES), 1)
    first = lane < SSM_HEADDIM

    dt_col = _softplus(dt_col_raw + bias_row)
    a_col = dt_col * (-LOG2E * jnp.exp(alog_row))
    hi, mid, lo = _split3(a_col)
    acum_col = _dot(tri, hi) + _dot(tri, mid) + _dot(tri, lo)
    if with_out:
        dt_row = _softplus(dt_row_raw + bias_col)
        a_row = dt_row * (-LOG2E * jnp.exp(alog_col))
        hi, mid, lo = _split3(a_row)
        acum_row = _dot_nt(hi, tri) + _dot_nt(mid, tri) + _dot_nt(lo, tri)

    def pair_cols(m, i):
        return jnp.where(first, m[:, 2 * i:2 * i + 1], m[:, 2 * i + 1:2 * i + 2])

    sn = sn_ref[...]
    ys, xws, dec = [], [], []
    for i in range(npair):
        dt_p = pair_cols(dt_col, i)
        ac_p = pair_cols(acum_col, i)
        xdt = xs[:, i * LANES:(i + 1) * LANES].astype(F32) * dt_p
        atot = ac_p[0:1, :] if rev else ac_p[t - 1:t, :]
        xws.append((xdt * jnp.exp2(atot - ac_p)).astype(BF16))
        dec.append(jnp.exp2(atot))
        if with_out:
            ws = []
            for j in (2 * i, 2 * i + 1):
                diff = jnp.minimum(acum_col[:, j:j + 1] - acum_row[j:j + 1, :], 0.0)
                ws.append((cb * jnp.exp2(diff)).astype(BF16))
            lhs = jnp.concatenate(ws, axis=1)
            zero = jnp.zeros_like(xdt)
            rhs = jnp.concatenate([jnp.where(first, xdt, zero), jnp.where(first, zero, xdt)],
                                  axis=0).astype(BF16)
            y_state = _dot(cc, sn[:, i * LANES:(i + 1) * LANES].astype(BF16)) * jnp.exp2(ac_p)
            ys.append(_dot(lhs, rhs) + y_state)
    xw = jnp.concatenate(xws, axis=1)
    sn_ref[...] = sn * jnp.concatenate(dec, axis=1) + _dot_tn(bc, xw)
    return jnp.concatenate(ys, axis=1) if with_out else None


SSD_UNROLL = 2


def _ssd_kernel(*refs, nc, with_out):
    if with_out:
        (xs_ref, b_ref, c_ref, dtc_ref, dtr_ref, biasr_ref, biasc_ref, alogr_ref, alogc_ref,
         dskip_ref, s0_ref, y_ref, st_out_ref, sn_ref, tri_ref, mask_ref) = refs
    else:
        (xs_ref, b_ref, dtc_ref, biasr_ref, alogr_ref, s0_ref, st_out_ref, sn_ref, tri_ref, mask_ref) = refs
        c_ref = dtr_ref = biasc_ref = alogc_ref = dskip_ref = y_ref = None
    _ssd_fill_masks(tri_ref, mask_ref)
    sn_ref[...] = s0_ref[...]
    unroll = SSD_UNROLL if nc % SSD_UNROLL == 0 else 1
    if with_out:
        dsk = dskip_ref[...]

        def init(ci, carry):
            rows = pl.ds(pl.multiple_of(ci * CHUNK, CHUNK), CHUNK)
            y_ref[rows, :] = xs_ref[rows, :].astype(F32) * dsk
            return carry

        lax.fori_loop(0, nc, init, 0)

    def body(ci, carry):
        for u in range(unroll):
            for d in (0, 1):
                step = ci * unroll + u
                cidx = step if d == 0 else nc - 1 - step
                rows = pl.ds(pl.multiple_of(cidx * CHUNK, CHUNK), CHUNK)
                xs = xs_ref[rows, :]
                bc = b_ref[rows, :]
                if with_out:
                    cc = c_ref[rows, :]
                    cb = _dot_nt(cc, bc) * mask_ref[d]
                    y = _ssd_chunk(xs, bc, cc, cb, dtc_ref[d, rows, :], dtr_ref[d, :, rows],
                                   biasr_ref[d], biasc_ref[d], alogr_ref[d], alogc_ref[d],
                                   sn_ref.at[d], tri_ref[d], d == 1, True)
                    y_ref[rows, :] += y
                else:
                    _ssd_chunk(xs, bc, None, None, dtc_ref[d, rows, :], None, biasr_ref[d], None,
                               alogr_ref[d], None, sn_ref.at[d], tri_ref[d], d == 1, False)
        return carry

    lax.fori_loop(0, nc // unroll, body, 0)
    st_out_ref[...] = sn_ref[...]


def _ssd_call(xbc, dt_col, dt_row, bias, alog, dskip, s0, with_out=True):
    bn, l, _ = xbc.shape
    g = SSM_GROUPS
    hg = SSM_HEADS // g
    hp = hg * SSM_HEADDIM
    d_inner = SSM_HEADS * SSM_HEADDIM
    nb_off = d_inner // LANES
    xs_spec = pl.BlockSpec((None, l, hp), lambda b, gi: (b, 0, gi))
    b_spec = pl.BlockSpec((None, l, LANES), lambda b, gi: (b, 0, nb_off + gi))
    c_spec = pl.BlockSpec((None, l, LANES), lambda b, gi: (b, 0, nb_off + g + gi))
    dtc_spec = pl.BlockSpec((None, None, 2, l, hg), lambda b, gi: (b, gi, 0, 0, 0))
    dtr_spec = pl.BlockSpec((None, None, 2, hg, l), lambda b, gi: (b, gi, 0, 0, 0))
    prow = pl.BlockSpec((None, 2, 1, hg), lambda b, gi: (gi, 0, 0, 0))
    pcol = pl.BlockSpec((None, 2, hg, 1), lambda b, gi: (gi, 0, 0, 0))
    dsk_spec = pl.BlockSpec((None, 1, hp), lambda b, gi: (gi, 0, 0))
    st_spec = pl.BlockSpec((None, None, 2, SSM_STATE, hp), lambda b, gi: (b, gi, 0, 0, 0))
    bias_r = bias.reshape(g, 2, 1, hg)
    bias_c = bias.reshape(g, 2, hg, 1)
    alog_r = alog.reshape(g, 2, 1, hg)
    alog_c = alog.reshape(g, 2, hg, 1)
    if with_out:
        in_specs = [xs_spec, b_spec, c_spec, dtc_spec, dtr_spec, prow, pcol, prow, pcol, dsk_spec, st_spec]
        args = [xbc, xbc, xbc, dt_col, dt_row, bias_r, bias_c, alog_r, alog_c, dskip, s0]
        out_specs = [xs_spec, st_spec]
        out_shape = [jax.ShapeDtypeStruct((bn, l, d_inner), F32), jax.ShapeDtypeStruct(s0.shape, F32)]
    else:
        in_specs = [xs_spec, b_spec, dtc_spec, prow, prow, st_spec]
        args = [xbc, xbc, dt_col, bias_r, alog_r, s0]
        out_specs = [st_spec]
        out_shape = [jax.ShapeDtypeStruct(s0.shape, F32)]
    res = pl.pallas_call(
        functools.partial(_ssd_kernel, nc=l // CHUNK, with_out=with_out),
        grid=(bn, g),
        in_specs=in_specs,
        out_specs=out_specs,
        out_shape=out_shape,
        scratch_shapes=[pltpu.VMEM((2, SSM_STATE, hp), F32), pltpu.VMEM((2, CHUNK, CHUNK), BF16),
                        pltpu.VMEM((2, CHUNK, CHUNK), F32)],
        compiler_params=_cparams("parallel", "parallel"),
        name="ssd",
    )(*args)
    return (res[0], res[1]) if with_out else (None, res[0])


def _merge_kernel(x_ref, oatt_ref, ohg_ref, hgate_ref, y_ref, z_ref, gates_ref, hgn_ref, ssn_ref,
                  gm_ref, wba_ref, wbh_ref, wbs_ref, wout_ref, o_ref):
    d = x_ref.shape[1]
    hgn = hgn_ref[...]
    parts = []
    for s in range(ohg_ref.shape[1] // LANES):
        sl = slice(s * LANES, (s + 1) * LANES)
        o = ohg_ref[:, sl]
        o = o * lax.rsqrt(jnp.mean(o * o, axis=1, keepdims=True) + EPS) * hgn
        parts.append((o * _silu(hgate_ref[:, sl].astype(F32))).astype(BF16))
    o_hg = jnp.concatenate(parts, axis=1)
    ys = y_ref[...] * _silu(z_ref[...].astype(F32))
    o_ssm = (ys * lax.rsqrt(jnp.mean(ys * ys, axis=1, keepdims=True) + EPS) * ssn_ref[...]).astype(BF16)
    g_att = _sigmoid(gates_ref[:, 0:d].astype(F32))
    g_hg = _sigmoid(gates_ref[:, d:2 * d].astype(F32))
    g_ssm = _sigmoid(gates_ref[:, 2 * d:3 * d].astype(F32))
    m = (g_att * _dot(oatt_ref[...], wba_ref[...]) + g_hg * _dot(o_hg, wbh_ref[...])
         + g_ssm * _dot(o_ssm, wbs_ref[...]))
    mix = _dot(m.astype(BF16), wout_ref[...])
    o_ref[...] = x_ref[...] + gm_ref[...] * mix


def _merge_call(x, o_att, o_hg, y, gz, hg_norm_g, ssm_norm_g, g_m, wba, wbh, wbs, wout):
    bn, l, d = x.shape
    tm = min(l, 256)
    di = y.shape[2]
    assert di == 2 * d and gz.shape[2] == 6 * d
    tok = lambda w, j=0: pl.BlockSpec((None, tm, w), lambda b, i: (b, i, j))
    full = lambda a: pl.BlockSpec(a.shape, lambda b, i: (0,) * a.ndim)
    hgn = hg_norm_g.reshape(1, LANES)
    ssn = ssm_norm_g.reshape(1, di)
    return pl.pallas_call(
        _merge_kernel,
        grid=(bn, l // tm),
        in_specs=[tok(d), tok(d), tok(d), tok(d, 2), tok(di), tok(di, 0), tok(N_BRANCH * d, 1), full(hgn), full(ssn),
                  pl.BlockSpec((None, 1, d), lambda b, i: (b, 0, 0)),
                  full(wba), full(wbh), full(wbs), full(wout)],
        out_specs=tok(d),
        out_shape=jax.ShapeDtypeStruct((bn, l, d), F32),
        compiler_params=_cparams("parallel", "parallel"),
        name="merge",
    )(x, o_att, o_hg, gz, y, gz, gz, hgn, ssn, g_m.reshape(bn, 1, d), wba, wbh, wbs, wout)


def _ffn_kernel(x_ref, g_ref, sh_ref, sc_ref, gf_ref, w1_ref, w3_ref, w2_ref, fin_ref, o_ref, *, final):
    x = x_ref[...]
    h = x * lax.rsqrt(jnp.mean(x * x, axis=-1, keepdims=True) + EPS) * g_ref[...]
    h = (h * (1.0 + sc_ref[...]) + sh_ref[...]).astype(BF16)
    u = _dot(h, w1_ref[...])
    t = _dot(h, w3_ref[...])
    a = (_silu(u) * t).astype(BF16)
    y = x + gf_ref[...] * _dot(a, w2_ref[...])
    if final:
        y = y * lax.rsqrt(jnp.mean(y * y, axis=-1, keepdims=True) + EPS) * fin_ref[...]
    o_ref[...] = y


def _ffn_call(x, g, shift, scale, gate, w1, w3, w2, final_g, final):
    bn, l, d = x.shape
    tm = min(l, 256)
    tok = pl.BlockSpec((None, tm, d), lambda b, i: (b, i, 0))
    vec = pl.BlockSpec((None, 1, d), lambda b, i: (b, 0, 0))
    row = pl.BlockSpec((1, d), lambda b, i: (0, 0))
    full = lambda a: pl.BlockSpec(a.shape, lambda b, i: (0, 0))
    return pl.pallas_call(
        functools.partial(_ffn_kernel, final=final),
        grid=(bn, l // tm),
        in_specs=[tok, row, vec, vec, vec, full(w1), full(w3), full(w2), row],
        out_specs=tok,
        out_shape=jax.ShapeDtypeStruct((bn, l, d), F32),
        compiler_params=_cparams("parallel", "parallel"),
        name="ffn",
    )(x, g.reshape(1, d), shift.reshape(bn, 1, d), scale.reshape(bn, 1, d), gate.reshape(bn, 1, d),
      w1, w3, w2, final_g.reshape(1, d))


def _rope_perm():
    idx = []
    for lane in range(LANES):
        half, mp, axis, i = lane // 64, (lane // 32) % 2, (lane // 16) % 2, lane % 16
        idx.append(mp * 64 + axis * 32 + half * 16 + i)
    return idx


def _rope_tables(l):
    rows = l // GRID_W
    row = jnp.repeat(jnp.arange(rows, dtype=F32), GRID_W)
    col = jnp.broadcast_to(jnp.arange(GRID_W, dtype=F32), (rows, GRID_W)).reshape(-1)
    nfreq = ATT_HD // 4
    freqs = ROPE_THETA ** (-jnp.arange(nfreq, dtype=F32) / nfreq)
    ang_r = row[:, None] * freqs[None, :]
    ang_c = col[:, None] * freqs[None, :]
    ang = jnp.concatenate([ang_r, ang_c, ang_r, ang_c], axis=1)
    ang = jnp.concatenate([ang, ang], axis=1)
    sign = jnp.where(jnp.arange(LANES) < LANES // 2, -1.0, 1.0).astype(F32)
    return jnp.cos(ang), jnp.sin(ang) * sign[None, :]


def _dt_layouts(dt_raw, bn, l):
    hg = SSM_HEADS // SSM_GROUPS
    t = dt_raw.reshape(bn, l, 2, SSM_GROUPS, hg)
    return t.transpose(0, 3, 2, 1, 4), t.transpose(0, 3, 2, 4, 1)


def kernel(x, c, ctx, c_ctx, w_ada, b_ada, norm1_g, w_in, att_lambda, att_norm_g, hg_lb_logits, hg_norm_g,
           ssm_conv_w, ssm_conv_b, ssm_dt_bias, ssm_a_log, ssm_d, ssm_norm_g, w_branch_att, w_branch_hg,
           w_branch_ssm, w_out, norm2_g, ffn_w1, ffn_w3, ffn_w2, final_g):
    bn, l, d = x.shape
    lc = ctx.shape[1]
    depth = w_ada.shape[0]
    att_w = ATT_HEADS * 2 * ATT_HD
    hg_w = HG_HEADS * LANES
    d_inner = SSM_HEADS * SSM_HEADDIM
    xbc_w = d_inner + 2 * SSM_GROUPS * SSM_STATE
    hgrp = SSM_HEADS // SSM_GROUPS

    sizes = (att_w, att_w, hg_w, hg_w, hg_w, xbc_w, SSM_HEADS, SSM_HEADS, att_w, hg_w, hg_w, d_inner, N_BRANCH * d)
    offs = [0]
    for s in sizes:
        offs.append(offs[-1] + s)
    (o_ak, o_av, o_hff, o_hfb, o_hi, o_xbc, o_dtf, o_dtb, o_aq, o_hq, o_hgate, o_z, o_gates, _) = offs

    perm = jnp.asarray([h * LANES + p for h in range(ATT_HEADS) for p in _rope_perm()], jnp.int32)
    cos_t, sin_t = _rope_tables(l)
    q_scale = ATT_HD ** -0.5 * math.log2(math.e)

    lb_w = jax.nn.softmax(hg_lb_logits.astype(F32), axis=1)
    lb = jnp.cumsum(lb_w, axis=1) - lb_w[:, :1]

    pad_rows = (-(bn + 1)) % SUBLANES
    c_rows = jnp.concatenate([c, c_ctx[None, :], jnp.zeros((pad_rows, d), F32)], axis=0)
    mod_all = _mod_call(c_rows, w_ada, b_ada)

    xc = ctx
    for li in range(depth):
        last = li == depth - 1
        mod = mod_all[li, :bn]
        mod_c = jnp.broadcast_to(mod_all[li, bn:bn + 1], (bn, 6 * d))
        sh_m, sc_m, g_m, sh_f, sc_f, g_f = [mod[:, i * d:(i + 1) * d] for i in range(6)]
        shc_m, scc_m, gc_m, shc_f, scc_f, gc_f = [mod_c[:, i * d:(i + 1) * d] for i in range(6)]

        lam_init = 0.8 - 0.6 * math.exp(-0.3 * li)
        al = att_lambda[li].astype(F32)
        lam = (jnp.exp(jnp.sum(al[0] * al[1])) - jnp.exp(jnp.sum(al[2] * al[3])) + lam_init).reshape(1)

        w = w_in[li]
        wb = lambda o, n: w[:, o:o + n].astype(BF16)
        w_q = wb(o_aq, att_w)[:, perm]
        w_k = wb(o_ak, att_w)[:, perm]
        w_vt = wb(o_av, att_w).T
        w_hi = wb(o_hi, hg_w)
        w_f = wb(o_hff, 2 * hg_w)
        w_hq = wb(o_hq, hg_w)
        w_xbc = wb(o_xbc, xbc_w)
        w_dt = jnp.concatenate([wb(o_dtf, 2 * SSM_HEADS), jnp.zeros((d, LANES - 2 * SSM_HEADS), BF16)], axis=1)
        w_gz = jnp.concatenate([wb(o_z, d_inner), wb(o_hgate, hg_w), wb(o_gates, N_BRANCH * d)], axis=1)

        lb_f = lb[0, li].reshape(1, hg_w)
        lb_b = lb[1, li].reshape(1, hg_w)
        bias_g = ssm_dt_bias[li].reshape(2, SSM_GROUPS, hgrp).transpose(1, 0, 2)
        alog_g = ssm_a_log[li].astype(F32).reshape(2, SSM_GROUPS, hgrp).transpose(1, 0, 2)
        dskip = jnp.repeat(ssm_d[li], SSM_HEADDIM).reshape(SSM_GROUPS, 1, hgrp * SSM_HEADDIM)
        wba = w_branch_att[li].astype(BF16)
        wbh = w_branch_hg[li].astype(BF16)
        wbs = w_branch_ssm[li].astype(BF16)
        wo = w_out[li].astype(BF16)
        w1 = ffn_w1[li].astype(BF16)
        w3 = ffn_w3[li].astype(BF16)
        w2 = ffn_w2[li].astype(BF16)

        hc = _adaln_call(xc, norm1_g[li], shc_m, scc_m)
        kc = _proj_call(hc, w_k, BF16, name="proj_kc")
        vtc = _projt_call(hc, w_vt, BF16, name="proj_vtc")
        vhc = _proj_call(hc, w_hi, BF16, name="proj_vhc")
        frc = _proj_call(hc, w_f, F32, name="proj_fc")
        xbcc = _conv_call(_proj_call(hc, w_xbc, BF16, name="proj_xbcc"), ssm_conv_w[li], ssm_conv_b[li])
        dtc = _proj_call(hc, w_dt, F32, name="proj_dtc")[:, :, :2 * SSM_HEADS]
        dtc_col, dtc_row = _dt_layouts(dtc, bn, lc)
        zero_g = jnp.zeros((bn, HG_HEADS, 2, LANES, LANES), F32)
        zero_s = jnp.zeros((bn, SSM_GROUPS, 2, SSM_STATE, hgrp * SSM_HEADDIM), F32)
        if last:
            _, sg_ctx = _gla_call(None, frc, vhc, 0, lb_f, lb_b, zero_g, with_out=False)
            _, ss_ctx = _ssd_call(xbcc, dtc_col, dtc_row, bias_g, alog_g, dskip, zero_s, with_out=False)
        else:
            qc = _proj_call(hc, w_q, BF16, epi="scale", scale=q_scale, name="proj_qc")
            hqc = _proj_call(hc, w_hq, BF16, epi="silu", name="proj_hqc")
            gzc = _proj_call(hc, w_gz, BF16, name="proj_gzc")
            oc_hg, sg_ctx = _gla_call(hqc, frc, vhc, 0, lb_f, lb_b, zero_g)
            yc, ss_ctx = _ssd_call(xbcc, dtc_col, dtc_row, bias_g, alog_g, dskip, zero_s)
            oc_att = _attn_call(lam, qc, [(kc, vtc)], att_norm_g[li], 1.0 - lam_init)

        h = _adaln_call(x, norm1_g[li], sh_m, sc_m)
        q = _proj_call(h, w_q, BF16, epi="rope", tables=(cos_t * q_scale, sin_t * q_scale), name="proj_q")
        k = _proj_call(h, w_k, BF16, epi="rope", tables=(cos_t, sin_t), name="proj_k")
        vt = _projt_call(h, w_vt, BF16, name="proj_vt")
        vh = _proj_call(h, w_hi, BF16, name="proj_vh")
        fr = _proj_call(h, w_f, F32, name="proj_f")
        hq = _proj_call(h, w_hq, BF16, epi="silu", name="proj_hq")
        xbc = _conv_call(_proj_call(h, w_xbc, BF16, name="proj_xbc"), ssm_conv_w[li], ssm_conv_b[li])
        dt = _proj_call(h, w_dt, F32, name="proj_dt")[:, :, :2 * SSM_HEADS]
        gz = _proj_call(h, w_gz, BF16, name="proj_gz")
        dt_col, dt_row = _dt_layouts(dt, bn, l)

        o_att = _attn_call(lam, q, [(kc, vtc), (k, vt)], att_norm_g[li], 1.0 - lam_init)
        o_hg, _ = _gla_call(hq, fr, vh, 0, lb_f, lb_b, sg_ctx)
        y, _ = _ssd_call(xbc, dt_col, dt_row, bias_g, alog_g, dskip, ss_ctx)

        x = _merge_call(x, o_att, o_hg, y, gz, hg_norm_g[li], ssm_norm_g[li], g_m, wba, wbh, wbs, wo)
        x = _ffn_call(x, norm2_g[li], sh_f, sc_f, g_f, w1, w3, w2, final_g, final=last)

        if not last:
            xc = _merge_call(xc, oc_att, oc_hg, yc, gzc, hg_norm_g[li], ssm_norm_g[li], gc_m,
                             wba, wbh, wbs, wo)
            xc = _ffn_call(xc, norm2_g[li], shc_f, scc_f, gc_f, w1, w3, w2, final_g, final=False)
    return x
```

```python
import functools
import math

import jax
import jax.numpy as jnp
from jax import lax
from jax.experimental import pallas as pl
from jax.experimental.pallas import tpu as pltpu

F32 = jnp.float32
BF16 = jnp.bfloat16

EPS = 1e-6
GRID_W = 64
ROPE_THETA = 10000.0
ATT_HEADS = 8
ATT_HD = 64
HG_HEADS = 8
SSM_HEADS = 32
SSM_HEADDIM = 64
SSM_GROUPS = 4
SSM_STATE = 128
SSM_CONV = 5
N_BRANCH = 3

LANES = 128
SUBLANES = 8
VMEM_LIMIT = 56 * 1024 * 1024

CHUNK = 128
DIAG = 8
LEVELS = (64, 32, 16, 8)


def _cparams(*sem):
    return pltpu.CompilerParams(dimension_semantics=sem, vmem_limit_bytes=VMEM_LIMIT)


def _sigmoid(x):
    return 1.0 / (1.0 + jnp.exp(-x))


def _silu(x):
    return x * _sigmoid(x)


def _split3(x):
    hi = x.astype(BF16)
    r = x - hi.astype(F32)
    mid = r.astype(BF16)
    lo = (r - mid.astype(F32)).astype(BF16)
    return hi, mid, lo


def _dot(a, b):
    return jnp.dot(a, b, preferred_element_type=F32)


def _dot_nt(a, b):
    return lax.dot_general(a, b, (((1,), (1,)), ((), ())), preferred_element_type=F32)


def _dot_tn(a, b):
    return lax.dot_general(a, b, (((0,), (0,)), ((), ())), preferred_element_type=F32)


def _mod_kernel(c_ref, w_ref, b_ref, o_ref):
    a = _silu(c_ref[...])
    a_hi = a.astype(BF16)
    a_lo = (a - a_hi.astype(F32)).astype(BF16)
    w = w_ref[...]
    w_hi = w.astype(BF16)
    w_lo = (w - w_hi.astype(F32)).astype(BF16)
    o_ref[...] = _dot(a_hi, w_hi) + _dot(a_lo, w_hi) + _dot(a_hi, w_lo) + b_ref[...]


def _mod_call(c_rows, w_ada, b_ada):
    depth, d, n = w_ada.shape
    r = c_rows.shape[0]
    tn = 1536
    return pl.pallas_call(
        _mod_kernel,
        grid=(depth, n // tn),
        in_specs=[
            pl.BlockSpec((r, d), lambda l, j: (0, 0)),
            pl.BlockSpec((None, d, tn), lambda l, j: (l, 0, j)),
            pl.BlockSpec((None, 1, tn), lambda l, j: (l, 0, j)),
        ],
        out_specs=pl.BlockSpec((None, r, tn), lambda l, j: (l, 0, j)),
        out_shape=jax.ShapeDtypeStruct((depth, r, n), F32),
        compiler_params=_cparams("parallel", "parallel"),
        name="mod",
    )(c_rows, w_ada, b_ada.reshape(depth, 1, n))


def _adaln_kernel(x_ref, g_ref, sh_ref, sc_ref, o_ref):
    x = x_ref[...]
    y = x * lax.rsqrt(jnp.mean(x * x, axis=-1, keepdims=True) + EPS) * g_ref[...]
    o_ref[...] = (y * (1.0 + sc_ref[...]) + sh_ref[...]).astype(o_ref.dtype)


def _adaln_call(x, g, shift, scale):
    bn, l, d = x.shape
    tl = min(l, 512)
    vec = pl.BlockSpec((None, 1, d), lambda b, i: (b, 0, 0))
    return pl.pallas_call(
        _adaln_kernel,
        grid=(bn, l // tl),
        in_specs=[
            pl.BlockSpec((None, tl, d), lambda b, i: (b, i, 0)),
            pl.BlockSpec((1, d), lambda b, i: (0, 0)),
            vec, vec,
        ],
        out_specs=pl.BlockSpec((None, tl, d), lambda b, i: (b, i, 0)),
        out_shape=jax.ShapeDtypeStruct((bn, l, d), BF16),
        compiler_params=_cparams("parallel", "parallel"),
        name="adaln",
    )(x, g.reshape(1, d), shift.reshape(bn, 1, d), scale.reshape(bn, 1, d))


def _proj_kernel(*refs, epi, scale):
    if epi == "rope":
        h_ref, w_ref, cos_ref, sin_ref, o_ref = refs
    else:
        h_ref, w_ref, o_ref = refs
    acc = _dot(h_ref[...], w_ref[...])
    if epi == "rope":
        cos = cos_ref[...]
        sin = sin_ref[...]
        for s in range(acc.shape[1] // LANES):
            t = acc[:, s * LANES:(s + 1) * LANES]
            r = t * cos + pltpu.roll(t, LANES // 2, 1) * sin
            o_ref[:, s * LANES:(s + 1) * LANES] = r.astype(o_ref.dtype)
    elif epi == "silu":
        o_ref[...] = _silu(acc).astype(o_ref.dtype)
    elif epi == "scale":
        o_ref[...] = (acc * scale).astype(o_ref.dtype)
    else:
        o_ref[...] = acc.astype(o_ref.dtype)


def _projt_kernel(h_ref, wt_ref, o_ref):
    o_ref[...] = _dot_nt(wt_ref[...], h_ref[...]).astype(o_ref.dtype)


def _projt_call(h, wt, out_dtype, name="projt"):
    bn, l, k = h.shape
    n = wt.shape[0]
    tm = min(l, 1024)
    tn = min(n, 1024)
    return pl.pallas_call(
        _projt_kernel,
        grid=(bn, l // tm, n // tn),
        in_specs=[
            pl.BlockSpec((None, tm, k), lambda b, i, j: (b, i, 0)),
            pl.BlockSpec((tn, k), lambda b, i, j: (j, 0)),
        ],
        out_specs=pl.BlockSpec((None, tn, tm), lambda b, i, j: (b, j, i)),
        out_shape=jax.ShapeDtypeStruct((bn, n, l), out_dtype),
        compiler_params=_cparams("parallel", "parallel", "parallel"),
        name=name,
    )(h, wt)


def _proj_call(h, w, out_dtype, epi="none", tables=None, scale=1.0, tn=None, name="proj"):
    bn, l, k = h.shape
    n = w.shape[1]
    tn = min(n, 1024) if tn is None else tn
    tm = min(l, 1024 if tn <= 1024 else 512)
    assert n % tn == 0 and l % tm == 0
    nl = l // tm
    in_specs = [
        pl.BlockSpec((None, tm, k), lambda b, i, j: (b, i, 0)),
        pl.BlockSpec((k, tn), lambda b, i, j: (0, j)),
    ]
    args = [h, w]
    if epi == "rope":
        tab = pl.BlockSpec((tm, LANES), lambda b, i, j: (i, 0))
        in_specs += [tab, tab]
        args += list(tables)
    return pl.pallas_call(
        functools.partial(_proj_kernel, epi=epi, scale=scale),
        grid=(bn, nl, n // tn),
        in_specs=in_specs,
        out_specs=pl.BlockSpec((None, tm, tn), lambda b, i, j: (b, i, j)),
        out_shape=jax.ShapeDtypeStruct((bn, l, n), out_dtype),
        compiler_params=_cparams("parallel", "parallel", "parallel"),
        name=name,
    )(*args)


def _conv_kernel(u_ref, w_ref, b_ref, o_ref, pad_ref, *, rows):
    l, cb = u_ref.shape
    halo = SUBLANES
    pad_ref[0:halo, :] = jnp.zeros((halo, cb), F32)
    pad_ref[halo + l:2 * halo + l, :] = jnp.zeros((halo, cb), F32)
    for r0 in range(0, l, rows):
        pad_ref[halo + r0:halo + r0 + rows, :] = u_ref[r0:r0 + rows, :].astype(F32)
    w = w_ref[...]
    bias = b_ref[...]
    half = SSM_CONV // 2
    n = rows + 2 * halo
    for r0 in range(0, l, rows):
        xa = pad_ref[r0:r0 + n, :]
        acc = bias + w[half:half + 1, :] * xa[halo:halo + rows, :]
        for k in range(SSM_CONV):
            if k == half:
                continue
            shifted = pltpu.roll(xa, n - halo - (k - half), 0)[:rows, :]
            acc = acc + w[k:k + 1, :] * shifted
        o_ref[r0:r0 + rows, :] = _silu(acc).astype(o_ref.dtype)


def _conv_call(u, w, b):
    bn, l, c = u.shape
    cb = 512
    rows = min(l, 512)
    return pl.pallas_call(
        functools.partial(_conv_kernel, rows=rows),
        grid=(bn, c // cb),
        in_specs=[
            pl.BlockSpec((None, l, cb), lambda b_, j: (b_, 0, j)),
            pl.BlockSpec((SSM_CONV, cb), lambda b_, j: (0, j)),
            pl.BlockSpec((1, cb), lambda b_, j: (0, j)),
        ],
        out_specs=pl.BlockSpec((None, l, cb), lambda b_, j: (b_, 0, j)),
        out_shape=jax.ShapeDtypeStruct((bn, l, c), BF16),
        scratch_shapes=[pltpu.VMEM((l + 2 * SUBLANES, cb), F32)],
        compiler_params=_cparams("parallel", "parallel"),
        name="conv",
    )(u, w, b.reshape(1, c))


ATT_TQ = 256
ATT_BLOCK_Q = 2048


def _attn_kernel(*refs, nsrc, post_scale):
    lam_ref, q_ref = refs[0], refs[1]
    k_refs = [refs[2 + 2 * i] for i in range(nsrc)]
    vt_refs = [refs[3 + 2 * i] for i in range(nsrc)]
    g_ref, o_ref = refs[2 + 2 * nsrc:4 + 2 * nsrc]
    s_scr = refs[4 + 2 * nsrc:]
    tq = ATT_TQ if q_ref.shape[0] % ATT_TQ == 0 else q_ref.shape[0]
    nt = q_ref.shape[0] // tq
    lane = lax.broadcasted_iota(jnp.int32, (tq, LANES), 1)
    second_map = ((lane // 32) % 2) == 1
    zero = jnp.zeros((tq, LANES), BF16)
    gain = g_ref[...] * post_scale
    lam = lam_ref[0]

    def key_chunks(sizes):
        out, off = [], 0
        for i in range(nsrc):
            lk = k_refs[i].shape[0]
            kc = next(s for s in sizes if lk % s == 0)
            out += [(i, c0, kc, off + c0) for c0 in range(0, lk, kc)]
            off += lk
        return out

    chunks = key_chunks((256, LANES))

    def stacked_q(t):
        q = q_ref[t * tq:(t + 1) * tq, :]
        return jnp.concatenate([jnp.where(second_map, zero, q), jnp.where(second_map, q, zero)], axis=0)

    def scores(qq, slot, chunk, mx):
        i, c0, kc, off = chunk
        s = _dot_nt(k_refs[i][c0:c0 + kc, :], qq)
        s_scr[slot][off:off + kc, :] = s
        cm = jnp.max(s.reshape(kc // SUBLANES, SUBLANES, 2 * tq), axis=0)
        return cm if mx is None else jnp.maximum(mx, cm)

    ones_rows = 2 * SUBLANES

    def probs_pv(slot, chunk, m, acc):
        i, c0, kc, off = chunk
        p = jnp.exp2(s_scr[slot][off:off + kc, :] - m).astype(BF16)
        vt1 = jnp.concatenate([vt_refs[i][:, c0:c0 + kc], jnp.ones((ones_rows, kc), BF16)], axis=0)
        part = _dot(vt1, p)
        return part if acc is None else acc + part

    qq = stacked_q(0)
    mx = None
    for chunk in chunks:
        mx = scores(qq, 0, chunk, mx)
    for t in range(nt):
        slot = t % 2
        m = jnp.max(mx, axis=0, keepdims=True)
        mx, acc = None, None
        if t + 1 < nt:
            qq = stacked_q(t + 1)
            for chunk in chunks:
                mx = scores(qq, 1 - slot, chunk, mx)
        for chunk in chunks:
            acc = probs_pv(slot, chunk, m, acc)
        den = acc[LANES:LANES + 1, :]
        c0 = 1.0 / den[:, :tq]
        c1 = lam / den[:, tq:]
        o = (acc[:LANES, :tq] * c0 - acc[:LANES, tq:] * c1).T
        o = o * lax.rsqrt(jnp.mean(o * o, axis=1, keepdims=True) + EPS) * gain
        o_ref[t * tq:(t + 1) * tq, :] = o.astype(o_ref.dtype)


def _attn_call(lam, q, kvs, g, post_scale):
    bn, lq, hw = q.shape
    nh = hw // LANES
    tq = min(lq, ATT_BLOCK_Q)
    in_specs = [
        pl.BlockSpec(memory_space=pltpu.SMEM),
        pl.BlockSpec((None, tq, LANES), lambda b, h, i: (b, i, h)),
    ]
    args = [lam, q]
    for k, vt in kvs:
        lk = k.shape[1]
        in_specs += [pl.BlockSpec((None, lk, LANES), lambda b, h, i: (b, 0, h)),
                     pl.BlockSpec((None, LANES, lk), lambda b, h, i: (b, h, 0))]
        args += [k, vt]
    in_specs.append(pl.BlockSpec((1, LANES), lambda b, h, i: (0, 0)))
    args.append(g.reshape(1, LANES))
    lk_total = sum(k.shape[1] for k, _ in kvs)
    sub = ATT_TQ if tq % ATT_TQ == 0 else tq
    return pl.pallas_call(
        functools.partial(_attn_kernel, nsrc=len(kvs), post_scale=post_scale),
        grid=(bn, nh, lq // tq),
        in_specs=in_specs,
        out_specs=pl.BlockSpec((None, tq, LANES), lambda b, h, i: (b, i, h)),
        out_shape=jax.ShapeDtypeStruct((bn, lq, hw), BF16),
        scratch_shapes=[pltpu.VMEM((lk_total, 2 * sub), F32), pltpu.VMEM((lk_total, 2 * sub), F32)],
        compiler_params=_cparams("parallel", "parallel", "parallel"),
        name="diff_attn",
    )(*args)


N_GLA_MASKS = len(LEVELS) + 1


def _gla_fill_masks(tri_ref, mask_ref):
    ti = lax.broadcasted_iota(jnp.int32, (CHUNK, CHUNK), 0)
    si = lax.broadcasted_iota(jnp.int32, (CHUNK, CHUNK), 1)
    x = ti ^ si
    for d in (0, 1):
        strict = (ti < si) if d else (ti > si)
        incl = (ti <= si) if d else (ti >= si)
        tri_ref[d] = jnp.where(incl, 1.0, 0.0).astype(BF16)
        for li, h in enumerate(LEVELS):
            mask_ref[d * N_GLA_MASKS + li] = jnp.where(strict & (x >= h) & (x < 2 * h), 1.0, 0.0)
        mask_ref[d * N_GLA_MASKS + len(LEVELS)] = jnp.where(incl & (x < DIAG), 1.0, 0.0)


def _gla_chunk(q, fr, v, lb, st_ref, b_ref, tri_ref, mask_ref, rev, with_out):
    c = CHUNK
    mbase = N_GLA_MASKS if rev else 0
    tri = tri_ref[1 if rev else 0]
    e = jnp.exp(-jnp.abs(fr))
    r = 1.0 / (1.0 + e)
    pos = fr >= 0.0
    sig = jnp.where(pos, r, e * r)
    nsig = jnp.where(pos, e * r, r)
    lf = jnp.log2(lb + (1.0 - lb) * sig)
    k = (1.0 - lb) * nsig
    hi, mid, lo = _split3(lf)
    b = _dot(tri, hi) + _dot(tri, mid) + _dot(tri, lo)
    b_ref[...] = b
    b_tot = b_ref[0:1, :] if rev else b_ref[c - 1:c, :]
    st = st_ref[...]
    out = None
    if with_out:
        qf = q.astype(F32)
        row = lax.broadcasted_iota(jnp.int32, (c, LANES), 0)
        a = None
        for li, h in enumerate(LEVELS):
            pieces = []
            for base in range(0, c, 2 * h):
                rr = base + h if rev else base + h - 1
                pieces.append(-jnp.abs(b_ref[base:base + 2 * h, :] - b_ref[rr:rr + 1, :]))
            ex = jnp.exp2(jnp.concatenate(pieces, axis=0) if len(pieces) > 1 else pieces[0])
            late = ((row & h) == 0) if rev else ((row & h) != 0)
            z = (jnp.where(late, qf, k) * ex).astype(BF16)
            term = mask_ref[mbase + li] * _dot_nt(z, z)
            a = term if a is None else a + term
        pieces = []
        for base in range(0, c, DIAG):
            rr = base + DIAG - 1 if rev else base
            pieces.append(jnp.abs(b_ref[base:base + DIAG, :] - b_ref[rr:rr + 1, :]))
        d = jnp.concatenate(pieces, axis=0)
        qd = (qf * jnp.exp2(-d)).astype(BF16)
        kd = (k * jnp.exp2(d)).astype(BF16)
        a = a + jnp.where(mask_ref[mbase + len(LEVELS)] > 0.5, _dot_nt(qd, kd), 0.0)
        out = _dot(a.astype(BF16), v) + _dot_nt((qf * jnp.exp2(b)).astype(BF16), st.astype(BF16))
    kw = (k * jnp.exp2(b_tot - b)).astype(BF16)
    st_ref[...] = st * jnp.exp2(b_tot) + _dot_tn(v, kw)
    return out


GLA_UNROLL = 4


def _gla_kernel(*refs, nc, with_out):
    if with_out:
        (q_ref, frf_ref, frb_ref, v_ref, lbf_ref, lbb_ref, s0_ref, o_ref, st_out_ref,
         st_ref, b_ref, tri_ref, mask_ref) = refs
    else:
        (frf_ref, frb_ref, v_ref, lbf_ref, lbb_ref, s0_ref, st_out_ref,
         st_ref, b_ref, tri_ref, mask_ref) = refs
        q_ref = o_ref = None
    _gla_fill_masks(tri_ref, mask_ref)
    st_ref[...] = s0_ref[...]
    if with_out:
        o_ref[...] = jnp.zeros(o_ref.shape, o_ref.dtype)
    unroll = math.gcd(nc, GLA_UNROLL)

    def body(ci, carry):
        for u in range(unroll):
            for d, (fr_ref, lb_ref) in enumerate(((frf_ref, lbf_ref), (frb_ref, lbb_ref))):
                step = ci * unroll + u
                cc = step if d == 0 else nc - 1 - step
                rows = pl.ds(pl.multiple_of(cc * CHUNK, CHUNK), CHUNK)
                q = q_ref[rows, :] if with_out else None
                out = _gla_chunk(q, fr_ref[rows, :], v_ref[rows, :], lb_ref[...], st_ref.at[d],
                                 b_ref.at[2 * u + d], tri_ref, mask_ref, d == 1, with_out)
                if with_out:
                    o_ref[rows, :] += out
        return carry

    lax.fori_loop(0, nc // unroll, body, 0)
    st_out_ref[...] = st_ref[...]


def _gla_call(q, fr, v, voff, lb_f, lb_b, s0, with_out=True):
    bn, l, _ = v.shape
    nh = HG_HEADS
    hw = nh * LANES
    seq = lambda off: pl.BlockSpec((None, l, LANES), lambda b, h: (b, 0, h + off))
    lbs = pl.BlockSpec((1, LANES), lambda b, h: (0, h))
    sts = pl.BlockSpec((None, None, 2, LANES, LANES), lambda b, h: (b, h, 0, 0, 0))
    in_specs = [seq(0), seq(nh), seq(voff), lbs, lbs, sts]
    args = [fr, fr, v, lb_f, lb_b, s0]
    out_specs = [sts]
    out_shape = [jax.ShapeDtypeStruct(s0.shape, F32)]
    if with_out:
        in_specs = [seq(0)] + in_specs
        args = [q] + args
        out_specs = [seq(0)] + out_specs
        out_shape = [jax.ShapeDtypeStruct((bn, l, hw), F32)] + out_shape
    res = pl.pallas_call(
        functools.partial(_gla_kernel, nc=l // CHUNK, with_out=with_out),
        grid=(bn, nh),
        in_specs=in_specs,
        out_specs=out_specs,
        out_shape=out_shape,
        scratch_shapes=[pltpu.VMEM((2, LANES, LANES), F32), pltpu.VMEM((2 * GLA_UNROLL, CHUNK, LANES), F32),
                        pltpu.VMEM((2, CHUNK, CHUNK), BF16), pltpu.VMEM((2 * N_GLA_MASKS, CHUNK, CHUNK), F32)],
        compiler_params=_cparams("parallel", "parallel"),
        name="hgrn2",
    )(*args)
    return (res[0], res[1]) if with_out else (None, res[0])


def _softplus(x):
    return jnp.maximum(x, 0.0) + jnp.log1p(jnp.exp(-jnp.abs(x)))


LOG2E = math.log2(math.e)


def _ssd_fill_masks(tri_ref, mask_ref):
    ti = lax.broadcasted_iota(jnp.int32, (CHUNK, CHUNK), 0)
    si = lax.broadcasted_iota(jnp.int32, (CHUNK, CHUNK), 1)
    for d in (0, 1):
        incl = jnp.where((ti <= si) if d else (ti >= si), 1.0, 0.0)
        mask_ref[d] = incl
        tri_ref[d] = incl.astype(BF16)


def _ssd_chunk(xs, bc, cc, cb, dt_col_raw, dt_row_raw, bias_row, bias_col, alog_row, alog_col,
               sn_ref, tri, rev, with_out):
    t = CHUNK
    hp = xs.shape[1]
    npair = hp // LANES
    lane = lax.broadcasted_iota(jnp.int32, (t, LANES), 1)
    first = lane < SSM_HEADDIM

    dt_col = _softplus(dt_col_raw + bias_row)
    a_col = dt_col * (-LOG2E * jnp.exp(alog_row))
    hi, mid, lo = _split3(a_col)
    acum_col = _dot(tri, hi) + _dot(tri, mid) + _dot(tri, lo)
    if with_out:
        dt_row = _softplus(dt_row_raw + bias_col)
        a_row = dt_row * (-LOG2E * jnp.exp(alog_col))
        hi, mid, lo = _split3(a_row)
        acum_row = _dot_nt(hi, tri) + _dot_nt(mid, tri) + _dot_nt(lo, tri)

    def pair_cols(m, i):
        return jnp.where(first, m[:, 2 * i:2 * i + 1], m[:, 2 * i + 1:2 * i + 2])

    sn = sn_ref[...]
    ys, xws, dec = [], [], []
    for i in range(npair):
        dt_p = pair_cols(dt_col, i)
        ac_p = pair_cols(acum_col, i)
        xdt = xs[:, i * LANES:(i + 1) * LANES].astype(F32) * dt_p
        atot = ac_p[0:1, :] if rev else ac_p[t - 1:t, :]
        xws.append((xdt * jnp.exp2(atot - ac_p)).astype(BF16))
        dec.append(jnp.exp2(atot))
        if with_out:
            ws = []
            for j in (2 * i, 2 * i + 1):
                diff = jnp.minimum(acum_col[:, j:j + 1] - acum_row[j:j + 1, :], 0.0)
                ws.append((cb * jnp.exp2(diff)).astype(BF16))
            lhs = jnp.concatenate(ws, axis=1)
            zero = jnp.zeros_like(xdt)
            rhs = jnp.concatenate([jnp.where(first, xdt, zero), jnp.where(first, zero, xdt)],
                                  axis=0).astype(BF16)
            y_state = _dot(cc, sn[:, i * LANES:(i + 1) * LANES].astype(BF16)) * jnp.exp2(ac_p)
            ys.append(_dot(lhs, rhs) + y_state)
    xw = jnp.concatenate(xws, axis=1)
    sn_ref[...] = sn * jnp.concatenate(dec, axis=1) + _dot_tn(bc, xw)
    return jnp.concatenate(ys, axis=1) if with_out else None


SSD_UNROLL = 2


def _ssd_kernel(*refs, nc, with_out):
    if with_out:
        (xs_ref, b_ref, c_ref, dtc_ref, dtr_ref, biasr_ref, biasc_ref, alogr_ref, alogc_ref,
         dskip_ref, s0_ref, y_ref, st_out_ref, sn_ref, tri_ref, mask_ref) = refs
    else:
        (xs_ref, b_ref, dtc_ref, biasr_ref, alogr_ref, s0_ref, st_out_ref, sn_ref, tri_ref, mask_ref) = refs
        c_ref = dtr_ref = biasc_ref = alogc_ref = dskip_ref = y_ref = None
    _ssd_fill_masks(tri_ref, mask_ref)
    sn_ref[...] = s0_ref[...]
    unroll = math.gcd(nc, SSD_UNROLL)
    if with_out:
        dsk = dskip_ref[...]

        def init(ci, carry):
            rows = pl.ds(pl.multiple_of(ci * CHUNK, CHUNK), CHUNK)
            y_ref[rows, :] = xs_ref[rows, :].astype(F32) * dsk
            return carry

        lax.fori_loop(0, nc, init, 0)

    def body(ci, carry):
        for u in range(unroll):
            for d in (0, 1):
                step = ci * unroll + u
                cidx = step if d == 0 else nc - 1 - step
                rows = pl.ds(pl.multiple_of(cidx * CHUNK, CHUNK), CHUNK)
                xs = xs_ref[rows, :]
                bc = b_ref[rows, :]
                if with_out:
                    cc = c_ref[rows, :]
                    cb = _dot_nt(cc, bc) * mask_ref[d]
                    y = _ssd_chunk(xs, bc, cc, cb, dtc_ref[d, rows, :], dtr_ref[d, :, rows],
                                   biasr_ref[d], biasc_ref[d], alogr_ref[d], alogc_ref[d],
                                   sn_ref.at[d], tri_ref[d], d == 1, True)
                    y_ref[rows, :] += y
                else:
                    _ssd_chunk(xs, bc, None, None, dtc_ref[d, rows, :], None, biasr_ref[d], None,
                               alogr_ref[d], None, sn_ref.at[d], tri_ref[d], d == 1, False)
        return carry

    lax.fori_loop(0, nc // unroll, body, 0)
    st_out_ref[...] = sn_ref[...]


def _ssd_call(xbc, dt_col, dt_row, bias, alog, dskip, s0, with_out=True):
    bn, l, _ = xbc.shape
    g = SSM_GROUPS
    hg = SSM_HEADS // g
    hp = hg * SSM_HEADDIM
    d_inner = SSM_HEADS * SSM_HEADDIM
    nb_off = d_inner // LANES
    xs_spec = pl.BlockSpec((None, l, hp), lambda b, gi: (b, 0, gi))
    b_spec = pl.BlockSpec((None, l, LANES), lambda b, gi: (b, 0, nb_off + gi))
    c_spec = pl.BlockSpec((None, l, LANES), lambda b, gi: (b, 0, nb_off + g + gi))
    dtc_spec = pl.BlockSpec((None, None, 2, l, hg), lambda b, gi: (b, gi, 0, 0, 0))
    dtr_spec = pl.BlockSpec((None, None, 2, hg, l), lambda b, gi: (b, gi, 0, 0, 0))
    prow = pl.BlockSpec((None, 2, 1, hg), lambda b, gi: (gi, 0, 0, 0))
    pcol = pl.BlockSpec((None, 2, hg, 1), lambda b, gi: (gi, 0, 0, 0))
    dsk_spec = pl.BlockSpec((None, 1, hp), lambda b, gi: (gi, 0, 0))
    st_spec = pl.BlockSpec((None, None, 2, SSM_STATE, hp), lambda b, gi: (b, gi, 0, 0, 0))
    bias_r = bias.reshape(g, 2, 1, hg)
    bias_c = bias.reshape(g, 2, hg, 1)
    alog_r = alog.reshape(g, 2, 1, hg)
    alog_c = alog.reshape(g, 2, hg, 1)
    if with_out:
        in_specs = [xs_spec, b_spec, c_spec, dtc_spec, dtr_spec, prow, pcol, prow, pcol, dsk_spec, st_spec]
        args = [xbc, xbc, xbc, dt_col, dt_row, bias_r, bias_c, alog_r, alog_c, dskip, s0]
        out_specs = [xs_spec, st_spec]
        out_shape = [jax.ShapeDtypeStruct((bn, l, d_inner), F32), jax.ShapeDtypeStruct(s0.shape, F32)]
    else:
        in_specs = [xs_spec, b_spec, dtc_spec, prow, prow, st_spec]
        args = [xbc, xbc, dt_col, bias_r, alog_r, s0]
        out_specs = [st_spec]
        out_shape = [jax.ShapeDtypeStruct(s0.shape, F32)]
    res = pl.pallas_call(
        functools.partial(_ssd_kernel, nc=l // CHUNK, with_out=with_out),
        grid=(bn, g),
        in_specs=in_specs,
        out_specs=out_specs,
        out_shape=out_shape,
        scratch_shapes=[pltpu.VMEM((2, SSM_STATE, hp), F32), pltpu.VMEM((2, CHUNK, CHUNK), BF16),
                        pltpu.VMEM((2, CHUNK, CHUNK), F32)],
        compiler_params=_cparams("parallel", "parallel"),
        name="ssd",
    )(*args)
    return (res[0], res[1]) if with_out else (None, res[0])


def _merge_kernel(x_ref, oatt_ref, ohg_ref, hgate_ref, y_ref, z_ref, gates_ref, hgn_ref, ssn_ref,
                  gm_ref, wba_ref, wbh_ref, wbs_ref, wout_ref, o_ref):
    d = x_ref.shape[1]
    hgn = hgn_ref[...]
    parts = []
    for s in range(ohg_ref.shape[1] // LANES):
        sl = slice(s * LANES, (s + 1) * LANES)
        o = ohg_ref[:, sl]
        o = o * lax.rsqrt(jnp.mean(o * o, axis=1, keepdims=True) + EPS) * hgn
        parts.append((o * _silu(hgate_ref[:, sl].astype(F32))).astype(BF16))
    o_hg = jnp.concatenate(parts, axis=1)
    ys = y_ref[...] * _silu(z_ref[...].astype(F32))
    o_ssm = (ys * lax.rsqrt(jnp.mean(ys * ys, axis=1, keepdims=True) + EPS) * ssn_ref[...]).astype(BF16)
    g_att = _sigmoid(gates_ref[:, 0:d].astype(F32))
    g_hg = _sigmoid(gates_ref[:, d:2 * d].astype(F32))
    g_ssm = _sigmoid(gates_ref[:, 2 * d:3 * d].astype(F32))
    m = (g_att * _dot(oatt_ref[...], wba_ref[...]) + g_hg * _dot(o_hg, wbh_ref[...])
         + g_ssm * _dot(o_ssm, wbs_ref[...]))
    mix = _dot(m.astype(BF16), wout_ref[...])
    o_ref[...] = x_ref[...] + gm_ref[...] * mix


def _merge_call(x, o_att, o_hg, y, gz, hg_norm_g, ssm_norm_g, g_m, wba, wbh, wbs, wout):
    bn, l, d = x.shape
    tm = min(l, 256)
    di = y.shape[2]
    assert di == 2 * d and gz.shape[2] == 6 * d
    tok = lambda w, j=0: pl.BlockSpec((None, tm, w), lambda b, i: (b, i, j))
    full = lambda a: pl.BlockSpec(a.shape, lambda b, i: (0,) * a.ndim)
    hgn = hg_norm_g.reshape(1, LANES)
    ssn = ssm_norm_g.reshape(1, di)
    return pl.pallas_call(
        _merge_kernel,
        grid=(bn, l // tm),
        in_specs=[tok(d), tok(d), tok(d), tok(d, 2), tok(di), tok(di, 0), tok(N_BRANCH * d, 1), full(hgn), full(ssn),
                  pl.BlockSpec((None, 1, d), lambda b, i: (b, 0, 0)),
                  full(wba), full(wbh), full(wbs), full(wout)],
        out_specs=tok(d),
        out_shape=jax.ShapeDtypeStruct((bn, l, d), F32),
        compiler_params=_cparams("parallel", "parallel"),
        name="merge",
    )(x, o_att, o_hg, gz, y, gz, gz, hgn, ssn, g_m.reshape(bn, 1, d), wba, wbh, wbs, wout)


def _ffn_kernel(x_ref, g_ref, sh_ref, sc_ref, gf_ref, w1_ref, w3_ref, w2_ref, fin_ref, o_ref, *, final):
    x = x_ref[...]
    h = x * lax.rsqrt(jnp.mean(x * x, axis=-1, keepdims=True) + EPS) * g_ref[...]
    h = (h * (1.0 + sc_ref[...]) + sh_ref[...]).astype(BF16)
    u = _dot(h, w1_ref[...])
    t = _dot(h, w3_ref[...])
    a = (_silu(u) * t).astype(BF16)
    y = x + gf_ref[...] * _dot(a, w2_ref[...])
    if final:
        y = y * lax.rsqrt(jnp.mean(y * y, axis=-1, keepdims=True) + EPS) * fin_ref[...]
    o_ref[...] = y


def _ffn_call(x, g, shift, scale, gate, w1, w3, w2, final_g, final):
    bn, l, d = x.shape
    tm = min(l, 256)
    tok = pl.BlockSpec((None, tm, d), lambda b, i: (b, i, 0))
    vec = pl.BlockSpec((None, 1, d), lambda b, i: (b, 0, 0))
    row = pl.BlockSpec((1, d), lambda b, i: (0, 0))
    full = lambda a: pl.BlockSpec(a.shape, lambda b, i: (0, 0))
    return pl.pallas_call(
        functools.partial(_ffn_kernel, final=final),
        grid=(bn, l // tm),
        in_specs=[tok, row, vec, vec, vec, full(w1), full(w3), full(w2), row],
        out_specs=tok,
        out_shape=jax.ShapeDtypeStruct((bn, l, d), F32),
        compiler_params=_cparams("parallel", "parallel"),
        name="ffn",
    )(x, g.reshape(1, d), shift.reshape(bn, 1, d), scale.reshape(bn, 1, d), gate.reshape(bn, 1, d),
      w1, w3, w2, final_g.reshape(1, d))


def _rope_perm():
    idx = []
    for lane in range(LANES):
        half, mp, axis, i = lane // 64, (lane // 32) % 2, (lane // 16) % 2, lane % 16
        idx.append(mp * 64 + axis * 32 + half * 16 + i)
    return idx


def _rope_tables(l):
    rows = l // GRID_W
    row = jnp.repeat(jnp.arange(rows, dtype=F32), GRID_W)
    col = jnp.broadcast_to(jnp.arange(GRID_W, dtype=F32), (rows, GRID_W)).reshape(-1)
    nfreq = ATT_HD // 4
    freqs = ROPE_THETA ** (-jnp.arange(nfreq, dtype=F32) / nfreq)
    ang_r = row[:, None] * freqs[None, :]
    ang_c = col[:, None] * freqs[None, :]
    ang = jnp.concatenate([ang_r, ang_c, ang_r, ang_c], axis=1)
    ang = jnp.concatenate([ang, ang], axis=1)
    sign = jnp.where(jnp.arange(LANES) < LANES // 2, -1.0, 1.0).astype(F32)
    return jnp.cos(ang), jnp.sin(ang) * sign[None, :]


def _dt_layouts(dt_raw, bn, l):
    hg = SSM_HEADS // SSM_GROUPS
    t = dt_raw.reshape(bn, l, 2, SSM_GROUPS, hg)
    return t.transpose(0, 3, 2, 1, 4), t.transpose(0, 3, 2, 4, 1)


def kernel(x, c, ctx, c_ctx, w_ada, b_ada, norm1_g, w_in, att_lambda, att_norm_g, hg_lb_logits, hg_norm_g,
           ssm_conv_w, ssm_conv_b, ssm_dt_bias, ssm_a_log, ssm_d, ssm_norm_g, w_branch_att, w_branch_hg,
           w_branch_ssm, w_out, norm2_g, ffn_w1, ffn_w3, ffn_w2, final_g):
    bn, l, d = x.shape
    lc = ctx.shape[1]
    depth = w_ada.shape[0]
    att_w = ATT_HEADS * 2 * ATT_HD
    hg_w = HG_HEADS * LANES
    d_inner = SSM_HEADS * SSM_HEADDIM
    xbc_w = d_inner + 2 * SSM_GROUPS * SSM_STATE
    hgrp = SSM_HEADS // SSM_GROUPS

    sizes = (att_w, att_w, hg_w, hg_w, hg_w, xbc_w, SSM_HEADS, SSM_HEADS, att_w, hg_w, hg_w, d_inner, N_BRANCH * d)
    offs = [0]
    for s in sizes:
        offs.append(offs[-1] + s)
    (o_ak, o_av, o_hff, o_hfb, o_hi, o_xbc, o_dtf, o_dtb, o_aq, o_hq, o_hgate, o_z, o_gates, _) = offs

    perm = jnp.asarray([h * LANES + p for h in range(ATT_HEADS) for p in _rope_perm()], jnp.int32)
    cos_t, sin_t = _rope_tables(l)
    q_scale = ATT_HD ** -0.5 * math.log2(math.e)

    lb_w = jax.nn.softmax(hg_lb_logits.astype(F32), axis=1)
    lb = jnp.cumsum(lb_w, axis=1) - lb_w[:, :1]

    pad_rows = (-(bn + 1)) % SUBLANES
    c_rows = jnp.concatenate([c, c_ctx[None, :], jnp.zeros((pad_rows, d), F32)], axis=0)
    mod_all = _mod_call(c_rows, w_ada, b_ada)

    xc = ctx
    for li in range(depth):
        last = li == depth - 1
        mod = mod_all[li, :bn]
        mod_c = jnp.broadcast_to(mod_all[li, bn:bn + 1], (bn, 6 * d))
        sh_m, sc_m, g_m, sh_f, sc_f, g_f = [mod[:, i * d:(i + 1) * d] for i in range(6)]
        shc_m, scc_m, gc_m, shc_f, scc_f, gc_f = [mod_c[:, i * d:(i + 1) * d] for i in range(6)]

        lam_init = 0.8 - 0.6 * math.exp(-0.3 * li)
        al = att_lambda[li].astype(F32)
        lam = (jnp.exp(jnp.sum(al[0] * al[1])) - jnp.exp(jnp.sum(al[2] * al[3])) + lam_init).reshape(1)

        w = w_in[li]
        wb = lambda o, n: w[:, o:o + n].astype(BF16)
        w_q = wb(o_aq, att_w)[:, perm]
        w_k = wb(o_ak, att_w)[:, perm]
        w_vt = wb(o_av, att_w).T
        w_hi = wb(o_hi, hg_w)
        w_f = jnp.concatenate([wb(o_hff, 2 * hg_w), wb(o_dtf, 2 * SSM_HEADS),
                               jnp.zeros((d, LANES - 2 * SSM_HEADS), BF16)], axis=1)
        w_hq = wb(o_hq, hg_w)
        w_xbc = wb(o_xbc, xbc_w)
        w_gz = jnp.concatenate([wb(o_z, d_inner), wb(o_hgate, hg_w), wb(o_gates, N_BRANCH * d)], axis=1)

        lb_f = lb[0, li].reshape(1, hg_w)
        lb_b = lb[1, li].reshape(1, hg_w)
        bias_g = ssm_dt_bias[li].reshape(2, SSM_GROUPS, hgrp).transpose(1, 0, 2)
        alog_g = ssm_a_log[li].astype(F32).reshape(2, SSM_GROUPS, hgrp).transpose(1, 0, 2)
        dskip = jnp.repeat(ssm_d[li], SSM_HEADDIM).reshape(SSM_GROUPS, 1, hgrp * SSM_HEADDIM)
        wba = w_branch_att[li].astype(BF16)
        wbh = w_branch_hg[li].astype(BF16)
        wbs = w_branch_ssm[li].astype(BF16)
        wo = w_out[li].astype(BF16)
        w1 = ffn_w1[li].astype(BF16)
        w3 = ffn_w3[li].astype(BF16)
        w2 = ffn_w2[li].astype(BF16)

        hc = _adaln_call(xc, norm1_g[li], shc_m, scc_m)
        kc = _proj_call(hc, w_k, BF16, name="proj_kc")
        vtc = _projt_call(hc, w_vt, BF16, name="proj_vtc")
        vhc = _proj_call(hc, w_hi, BF16, name="proj_vhc")
        frc = _proj_call(hc, w_f, F32, tn=w_f.shape[1], name="proj_fc")
        xbcc = _conv_call(_proj_call(hc, w_xbc, BF16, name="proj_xbcc"), ssm_conv_w[li], ssm_conv_b[li])
        dtc = frc[:, :, 2 * hg_w:2 * hg_w + 2 * SSM_HEADS]
        dtc_col, dtc_row = _dt_layouts(dtc, bn, lc)
        zero_g = jnp.zeros((bn, HG_HEADS, 2, LANES, LANES), F32)
        zero_s = jnp.zeros((bn, SSM_GROUPS, 2, SSM_STATE, hgrp * SSM_HEADDIM), F32)
        if last:
            _, sg_ctx = _gla_call(None, frc, vhc, 0, lb_f, lb_b, zero_g, with_out=False)
            _, ss_ctx = _ssd_call(xbcc, dtc_col, dtc_row, bias_g, alog_g, dskip, zero_s, with_out=False)
        else:
            qc = _proj_call(hc, w_q, BF16, epi="scale", scale=q_scale, name="proj_qc")
            hqc = _proj_call(hc, w_hq, BF16, epi="silu", name="proj_hqc")
            gzc = _proj_call(hc, w_gz, BF16, name="proj_gzc")
            oc_hg, sg_ctx = _gla_call(hqc, frc, vhc, 0, lb_f, lb_b, zero_g)
            yc, ss_ctx = _ssd_call(xbcc, dtc_col, dtc_row, bias_g, alog_g, dskip, zero_s)
            oc_att = _attn_call(lam, qc, [(kc, vtc)], att_norm_g[li], 1.0 - lam_init)

        h = _adaln_call(x, norm1_g[li], sh_m, sc_m)
        q = _proj_call(h, w_q, BF16, epi="rope", tables=(cos_t * q_scale, sin_t * q_scale), name="proj_q")
        k = _proj_call(h, w_k, BF16, epi="rope", tables=(cos_t, sin_t), name="proj_k")
        vt = _projt_call(h, w_vt, BF16, name="proj_vt")
        vh = _proj_call(h, w_hi, BF16, name="proj_vh")
        fr = _proj_call(h, w_f, F32, tn=w_f.shape[1], name="proj_f")
        hq = _proj_call(h, w_hq, BF16, epi="silu", name="proj_hq")
        xbc = _conv_call(_proj_call(h, w_xbc, BF16, name="proj_xbc"), ssm_conv_w[li], ssm_conv_b[li])
        dt = fr[:, :, 2 * hg_w:2 * hg_w + 2 * SSM_HEADS]
        gz = _proj_call(h, w_gz, BF16, name="proj_gz")
        dt_col, dt_row = _dt_layouts(dt, bn, l)

        o_att = _attn_call(lam, q, [(kc, vtc), (k, vt)], att_norm_g[li], 1.0 - lam_init)
        o_hg, _ = _gla_call(hq, fr, vh, 0, lb_f, lb_b, sg_ctx)
        y, _ = _ssd_call(xbc, dt_col, dt_row, bias_g, alog_g, dskip, ss_ctx)

        x = _merge_call(x, o_att, o_hg, y, gz, hg_norm_g[li], ssm_norm_g[li], g_m, wba, wbh, wbs, wo)
        x = _ffn_call(x, norm2_g[li], sh_f, sc_f, g_f, w1, w3, w2, final_g, final=last)

        if not last:
            xc = _merge_call(xc, oc_att, oc_hg, yc, gzc, hg_norm_g[li], ssm_norm_g[li], gc_m,
                             wba, wbh, wbs, wo)
            xc = _ffn_call(xc, norm2_g[li], shc_f, scc_f, gc_f, w1, w3, w2, final_g, final=False)
    return x
```

```python
import functools
import math

import jax
import jax.numpy as jnp
from jax import lax
from jax.experimental import pallas as pl
from jax.experimental.pallas import tpu as pltpu

F32 = jnp.float32
BF16 = jnp.bfloat16

EPS = 1e-6
GRID_W = 64
ROPE_THETA = 10000.0
ATT_HEADS = 8
ATT_HD = 64
HG_HEADS = 8
SSM_HEADS = 32
SSM_HEADDIM = 64
SSM_GROUPS = 4
SSM_STATE = 128
SSM_CONV = 5
N_BRANCH = 3

LANES = 128
SUBLANES = 8
VMEM_LIMIT = 56 * 1024 * 1024

DENSE_TM = 512
CHUNK = 128
DIAG = 8
LEVELS = (64, 32, 16, 8)


def _cparams(*sem):
    return pltpu.CompilerParams(dimension_semantics=sem, vmem_limit_bytes=VMEM_LIMIT)


def _sigmoid(x):
    return 0.5 * jnp.tanh(0.5 * x) + 0.5


def _silu(x):
    hx = 0.5 * x
    return hx * jnp.tanh(hx) + hx


def _split3(x):
    hi = x.astype(BF16)
    r = x - hi.astype(F32)
    mid = r.astype(BF16)
    lo = (r - mid.astype(F32)).astype(BF16)
    return hi, mid, lo


def _dot(a, b):
    return jnp.dot(a, b, preferred_element_type=F32)


def _dot_nt(a, b):
    return lax.dot_general(a, b, (((1,), (1,)), ((), ())), preferred_element_type=F32)


def _dot_tn(a, b):
    return lax.dot_general(a, b, (((0,), (0,)), ((), ())), preferred_element_type=F32)


def _mod_kernel(c_ref, w_ref, b_ref, o_ref):
    a = _silu(c_ref[...])
    a_hi = a.astype(BF16)
    a_lo = (a - a_hi.astype(F32)).astype(BF16)
    w = w_ref[...]
    w_hi = w.astype(BF16)
    w_lo = (w - w_hi.astype(F32)).astype(BF16)
    o_ref[...] = _dot(a_hi, w_hi) + _dot(a_lo, w_hi) + _dot(a_hi, w_lo) + b_ref[...]


def _mod_call(c_rows, w_ada, b_ada):
    depth, d, n = w_ada.shape
    r = c_rows.shape[0]
    tn = 1536
    return pl.pallas_call(
        _mod_kernel,
        grid=(depth, n // tn),
        in_specs=[
            pl.BlockSpec((r, d), lambda l, j: (0, 0)),
            pl.BlockSpec((None, d, tn), lambda l, j: (l, 0, j)),
            pl.BlockSpec((None, 1, tn), lambda l, j: (l, 0, j)),
        ],
        out_specs=pl.BlockSpec((None, r, tn), lambda l, j: (l, 0, j)),
        out_shape=jax.ShapeDtypeStruct((depth, r, n), F32),
        compiler_params=_cparams("parallel", "parallel"),
        name="mod",
    )(c_rows, w_ada, b_ada.reshape(depth, 1, n))


def _adaln_kernel(x_ref, g_ref, sh_ref, sc_ref, o_ref):
    x = x_ref[...]
    y = x * lax.rsqrt(jnp.mean(x * x, axis=-1, keepdims=True) + EPS) * g_ref[...]
    o_ref[...] = (y * (1.0 + sc_ref[...]) + sh_ref[...]).astype(o_ref.dtype)


def _adaln_call(x, g, shift, scale):
    bn, l, d = x.shape
    tl = min(l, 512)
    vec = pl.BlockSpec((None, 1, d), lambda b, i: (b, 0, 0))
    return pl.pallas_call(
        _adaln_kernel,
        grid=(bn, l // tl),
        in_specs=[
            pl.BlockSpec((None, tl, d), lambda b, i: (b, i, 0)),
            pl.BlockSpec((1, d), lambda b, i: (0, 0)),
            vec, vec,
        ],
        out_specs=pl.BlockSpec((None, tl, d), lambda b, i: (b, i, 0)),
        out_shape=jax.ShapeDtypeStruct((bn, l, d), BF16),
        compiler_params=_cparams("parallel", "parallel"),
        name="adaln",
    )(x, g.reshape(1, d), shift.reshape(bn, 1, d), scale.reshape(bn, 1, d))


def _proj_kernel(*refs, epi, scale):
    if epi == "rope":
        h_ref, w_ref, cos_ref, sin_ref, o_ref = refs
    else:
        h_ref, w_ref, o_ref = refs
    acc = _dot(h_ref[...], w_ref[...])
    if epi == "rope":
        cos = cos_ref[...]
        sin = sin_ref[...]
        for s in range(acc.shape[1] // LANES):
            t = acc[:, s * LANES:(s + 1) * LANES]
            r = t * cos + pltpu.roll(t, LANES // 2, 1) * sin
            o_ref[:, s * LANES:(s + 1) * LANES] = r.astype(o_ref.dtype)
    elif epi == "silu":
        o_ref[...] = _silu(acc).astype(o_ref.dtype)
    elif epi == "scale":
        o_ref[...] = (acc * scale).astype(o_ref.dtype)
    else:
        o_ref[...] = acc.astype(o_ref.dtype)


def _projt_kernel(h_ref, wt_ref, o_ref):
    o_ref[...] = _dot_nt(wt_ref[...], h_ref[...]).astype(o_ref.dtype)


def _projt_call(h, wt, out_dtype, name="projt"):
    bn, l, k = h.shape
    n = wt.shape[0]
    tm = min(l, 1024)
    tn = min(n, 1024)
    return pl.pallas_call(
        _projt_kernel,
        grid=(bn, l // tm, n // tn),
        in_specs=[
            pl.BlockSpec((None, tm, k), lambda b, i, j: (b, i, 0)),
            pl.BlockSpec((tn, k), lambda b, i, j: (j, 0)),
        ],
        out_specs=pl.BlockSpec((None, tn, tm), lambda b, i, j: (b, j, i)),
        out_shape=jax.ShapeDtypeStruct((bn, n, l), out_dtype),
        compiler_params=_cparams("parallel", "parallel", "parallel"),
        name=name,
    )(h, wt)


def _proj_call(h, w, out_dtype, epi="none", tables=None, scale=1.0, tn=None, name="proj"):
    bn, l, k = h.shape
    n = w.shape[1]
    tn = min(n, 1024) if tn is None else tn
    tm = min(l, 1024 if tn <= 1024 else 512)
    assert n % tn == 0 and l % tm == 0
    nl = l // tm
    in_specs = [
        pl.BlockSpec((None, tm, k), lambda b, i, j: (b, i, 0)),
        pl.BlockSpec((k, tn), lambda b, i, j: (0, j)),
    ]
    args = [h, w]
    if epi == "rope":
        tab = pl.BlockSpec((tm, LANES), lambda b, i, j: (i, 0))
        in_specs += [tab, tab]
        args += list(tables)
    return pl.pallas_call(
        functools.partial(_proj_kernel, epi=epi, scale=scale),
        grid=(bn, nl, n // tn),
        in_specs=in_specs,
        out_specs=pl.BlockSpec((None, tm, tn), lambda b, i, j: (b, i, j)),
        out_shape=jax.ShapeDtypeStruct((bn, l, n), out_dtype),
        compiler_params=_cparams("parallel", "parallel", "parallel"),
        name=name,
    )(*args)


def _conv_kernel(u_ref, w_ref, b_ref, o_ref, pad_ref, *, rows):
    l, cb = u_ref.shape
    halo = SUBLANES
    pad_ref[0:halo, :] = jnp.zeros((halo, cb), F32)
    pad_ref[halo + l:2 * halo + l, :] = jnp.zeros((halo, cb), F32)
    for r0 in range(0, l, rows):
        pad_ref[halo + r0:halo + r0 + rows, :] = u_ref[r0:r0 + rows, :].astype(F32)
    w = w_ref[...]
    bias = b_ref[...]
    half = SSM_CONV // 2
    n = rows + 2 * halo
    for r0 in range(0, l, rows):
        xa = pad_ref[r0:r0 + n, :]
        acc = bias + w[half:half + 1, :] * xa[halo:halo + rows, :]
        for k in range(SSM_CONV):
            if k == half:
                continue
            shifted = pltpu.roll(xa, n - halo - (k - half), 0)[:rows, :]
            acc = acc + w[k:k + 1, :] * shifted
        o_ref[r0:r0 + rows, :] = _silu(acc).astype(o_ref.dtype)


def _conv_call(u, w, b):
    bn, l, c = u.shape
    cb = 512
    rows = min(l, 512)
    return pl.pallas_call(
        functools.partial(_conv_kernel, rows=rows),
        grid=(bn, c // cb),
        in_specs=[
            pl.BlockSpec((None, l, cb), lambda b_, j: (b_, 0, j)),
            pl.BlockSpec((SSM_CONV, cb), lambda b_, j: (0, j)),
            pl.BlockSpec((1, cb), lambda b_, j: (0, j)),
        ],
        out_specs=pl.BlockSpec((None, l, cb), lambda b_, j: (b_, 0, j)),
        out_shape=jax.ShapeDtypeStruct((bn, l, c), BF16),
        scratch_shapes=[pltpu.VMEM((l + 2 * SUBLANES, cb), F32)],
        compiler_params=_cparams("parallel", "parallel"),
        name="conv",
    )(u, w, b.reshape(1, c))


ATT_TQ = 256
ATT_BLOCK_Q = 2048


def _attn_kernel(*refs, nsrc, post_scale):
    lam_ref, q_ref = refs[0], refs[1]
    k_refs = [refs[2 + 2 * i] for i in range(nsrc)]
    vt_refs = [refs[3 + 2 * i] for i in range(nsrc)]
    g_ref, o_ref = refs[2 + 2 * nsrc:4 + 2 * nsrc]
    s_scr = refs[4 + 2 * nsrc:]
    tq = ATT_TQ if q_ref.shape[0] % ATT_TQ == 0 else q_ref.shape[0]
    nt = q_ref.shape[0] // tq
    lane = lax.broadcasted_iota(jnp.int32, (tq, LANES), 1)
    second_map = ((lane // 32) % 2) == 1
    zero = jnp.zeros((tq, LANES), BF16)
    gain = g_ref[...] * post_scale
    lam = lam_ref[0]

    def key_chunks(sizes):
        out, off = [], 0
        for i in range(nsrc):
            lk = k_refs[i].shape[0]
            kc = next(s for s in sizes if lk % s == 0)
            out += [(i, c0, kc, off + c0) for c0 in range(0, lk, kc)]
            off += lk
        return out

    chunks = key_chunks((256, LANES))

    def stacked_q(t):
        q = q_ref[t * tq:(t + 1) * tq, :]
        return jnp.concatenate([jnp.where(second_map, zero, q), jnp.where(second_map, q, zero)], axis=0)

    def scores(qq, slot, chunk, mx):
        i, c0, kc, off = chunk
        s = _dot_nt(k_refs[i][c0:c0 + kc, :], qq)
        s_scr[slot][off:off + kc, :] = s
        cm = jnp.max(s.reshape(kc // SUBLANES, SUBLANES, 2 * tq), axis=0)
        return cm if mx is None else jnp.maximum(mx, cm)

    ones_rows = 2 * SUBLANES

    def probs_pv(slot, chunk, m, acc):
        i, c0, kc, off = chunk
        p = jnp.exp2(s_scr[slot][off:off + kc, :] - m).astype(BF16)
        vt1 = jnp.concatenate([vt_refs[i][:, c0:c0 + kc], jnp.ones((ones_rows, kc), BF16)], axis=0)
        part = _dot(vt1, p)
        return part if acc is None else acc + part

    qq = stacked_q(0)
    mx = None
    for chunk in chunks:
        mx = scores(qq, 0, chunk, mx)
    for t in range(nt):
        slot = t % 2
        m = jnp.max(mx, axis=0, keepdims=True)
        mx, acc = None, None
        if t + 1 < nt:
            qq = stacked_q(t + 1)
            for chunk in chunks:
                mx = scores(qq, 1 - slot, chunk, mx)
        for chunk in chunks:
            acc = probs_pv(slot, chunk, m, acc)
        den = acc[LANES:LANES + 1, :]
        c0 = 1.0 / den[:, :tq]
        c1 = lam / den[:, tq:]
        o = (acc[:LANES, :tq] * c0 - acc[:LANES, tq:] * c1).T
        o = o * lax.rsqrt(jnp.mean(o * o, axis=1, keepdims=True) + EPS) * gain
        o_ref[t * tq:(t + 1) * tq, :] = o.astype(o_ref.dtype)


def _attn_call(lam, q, kvs, g, post_scale):
    bn, lq, hw = q.shape
    nh = hw // LANES
    tq = min(lq, ATT_BLOCK_Q)
    in_specs = [
        pl.BlockSpec(memory_space=pltpu.SMEM),
        pl.BlockSpec((None, tq, LANES), lambda b, h, i: (b, i, h)),
    ]
    args = [lam, q]
    for k, vt in kvs:
        lk = k.shape[1]
        in_specs += [pl.BlockSpec((None, lk, LANES), lambda b, h, i: (b, 0, h)),
                     pl.BlockSpec((None, LANES, lk), lambda b, h, i: (b, h, 0))]
        args += [k, vt]
    in_specs.append(pl.BlockSpec((1, LANES), lambda b, h, i: (0, 0)))
    args.append(g.reshape(1, LANES))
    lk_total = sum(k.shape[1] for k, _ in kvs)
    sub = ATT_TQ if tq % ATT_TQ == 0 else tq
    return pl.pallas_call(
        functools.partial(_attn_kernel, nsrc=len(kvs), post_scale=post_scale),
        grid=(bn, nh, lq // tq),
        in_specs=in_specs,
        out_specs=pl.BlockSpec((None, tq, LANES), lambda b, h, i: (b, i, h)),
        out_shape=jax.ShapeDtypeStruct((bn, lq, hw), BF16),
        scratch_shapes=[pltpu.VMEM((lk_total, 2 * sub), F32), pltpu.VMEM((lk_total, 2 * sub), F32)],
        compiler_params=_cparams("parallel", "parallel", "parallel"),
        name="diff_attn",
    )(*args)


N_GLA_MASKS = len(LEVELS) + 1


def _gla_fill_masks(tri_ref, mask_ref):
    ti = lax.broadcasted_iota(jnp.int32, (CHUNK, CHUNK), 0)
    si = lax.broadcasted_iota(jnp.int32, (CHUNK, CHUNK), 1)
    x = ti ^ si
    for d in (0, 1):
        strict = (ti < si) if d else (ti > si)
        incl = (ti <= si) if d else (ti >= si)
        tri_ref[d] = jnp.where(incl, 1.0, 0.0).astype(BF16)
        for li, h in enumerate(LEVELS):
            mask_ref[d * N_GLA_MASKS + li] = jnp.where(strict & (x >= h) & (x < 2 * h), 1.0, 0.0)
        mask_ref[d * N_GLA_MASKS + len(LEVELS)] = jnp.where(incl & (x < DIAG), 1.0, 0.0)


def _gla_chunk(q, fr, v, lb, st_ref, b_ref, tri_ref, mask_ref, rev, with_out):
    c = CHUNK
    mbase = N_GLA_MASKS if rev else 0
    tri = tri_ref[1 if rev else 0]
    e = jnp.exp(-jnp.abs(fr))
    r = 1.0 / (1.0 + e)
    pos = fr >= 0.0
    sig = jnp.where(pos, r, e * r)
    nsig = jnp.where(pos, e * r, r)
    lf = jnp.log2(lb + (1.0 - lb) * sig)
    k = (1.0 - lb) * nsig
    hi, mid, lo = _split3(lf)
    b = _dot(tri, hi) + _dot(tri, mid) + _dot(tri, lo)
    b_ref[...] = b
    b_tot = b_ref[0:1, :] if rev else b_ref[c - 1:c, :]
    st = st_ref[...]
    out = None
    if with_out:
        qf = q.astype(F32)
        row = lax.broadcasted_iota(jnp.int32, (c, LANES), 0)
        a = None
        for li, h in enumerate(LEVELS):
            pieces = []
            for base in range(0, c, 2 * h):
                rr = base + h if rev else base + h - 1
                pieces.append(-jnp.abs(b_ref[base:base + 2 * h, :] - b_ref[rr:rr + 1, :]))
            ex = jnp.exp2(jnp.concatenate(pieces, axis=0) if len(pieces) > 1 else pieces[0])
            late = ((row & h) == 0) if rev else ((row & h) != 0)
            z = (jnp.where(late, qf, k) * ex).astype(BF16)
            term = mask_ref[mbase + li] * _dot_nt(z, z)
            a = term if a is None else a + term
        pieces = []
        for base in range(0, c, DIAG):
            rr = base + DIAG - 1 if rev else base
            pieces.append(jnp.abs(b_ref[base:base + DIAG, :] - b_ref[rr:rr + 1, :]))
        d = jnp.concatenate(pieces, axis=0)
        qd = (qf * jnp.exp2(-d)).astype(BF16)
        kd = (k * jnp.exp2(d)).astype(BF16)
        a = a + jnp.where(mask_ref[mbase + len(LEVELS)] > 0.5, _dot_nt(qd, kd), 0.0)
        out = _dot(a.astype(BF16), v) + _dot_nt((qf * jnp.exp2(b)).astype(BF16), st.astype(BF16))
    kw = (k * jnp.exp2(b_tot - b)).astype(BF16)
    st_ref[...] = st * jnp.exp2(b_tot) + _dot_tn(v, kw)
    return out


GLA_UNROLL = 4


def _gla_kernel(*refs, nc, with_out):
    if with_out:
        (q_ref, frf_ref, frb_ref, v_ref, lbf_ref, lbb_ref, s0_ref, o_ref, st_out_ref,
         st_ref, b_ref, tri_ref, mask_ref) = refs
    else:
        (frf_ref, frb_ref, v_ref, lbf_ref, lbb_ref, s0_ref, st_out_ref,
         st_ref, b_ref, tri_ref, mask_ref) = refs
        q_ref = o_ref = None
    _gla_fill_masks(tri_ref, mask_ref)
    st_ref[...] = s0_ref[...]
    if with_out:
        o_ref[...] = jnp.zeros(o_ref.shape, o_ref.dtype)
    unroll = math.gcd(nc, GLA_UNROLL)

    def body(ci, carry):
        for u in range(unroll):
            for d, (fr_ref, lb_ref) in enumerate(((frf_ref, lbf_ref), (frb_ref, lbb_ref))):
                step = ci * unroll + u
                cc = step if d == 0 else nc - 1 - step
                rows = pl.ds(pl.multiple_of(cc * CHUNK, CHUNK), CHUNK)
                q = q_ref[rows, :] if with_out else None
                out = _gla_chunk(q, fr_ref[rows, :], v_ref[rows, :], lb_ref[...], st_ref.at[d],
                                 b_ref.at[2 * u + d], tri_ref, mask_ref, d == 1, with_out)
                if with_out:
                    o_ref[rows, :] += out
        return carry

    lax.fori_loop(0, nc // unroll, body, 0)
    st_out_ref[...] = st_ref[...]


def _gla_call(q, fr, v, voff, lb_f, lb_b, s0, with_out=True):
    bn, l, _ = v.shape
    nh = HG_HEADS
    hw = nh * LANES
    seq = lambda off: pl.BlockSpec((None, l, LANES), lambda b, h: (b, 0, h + off))
    lbs = pl.BlockSpec((1, LANES), lambda b, h: (0, h))
    sts = pl.BlockSpec((None, None, 2, LANES, LANES), lambda b, h: (b, h, 0, 0, 0))
    in_specs = [seq(0), seq(nh), seq(voff), lbs, lbs, sts]
    args = [fr, fr, v, lb_f, lb_b, s0]
    out_specs = [sts]
    out_shape = [jax.ShapeDtypeStruct(s0.shape, F32)]
    if with_out:
        in_specs = [seq(0)] + in_specs
        args = [q] + args
        out_specs = [seq(0)] + out_specs
        out_shape = [jax.ShapeDtypeStruct((bn, l, hw), F32)] + out_shape
    res = pl.pallas_call(
        functools.partial(_gla_kernel, nc=l // CHUNK, with_out=with_out),
        grid=(bn, nh),
        in_specs=in_specs,
        out_specs=out_specs,
        out_shape=out_shape,
        scratch_shapes=[pltpu.VMEM((2, LANES, LANES), F32), pltpu.VMEM((2 * GLA_UNROLL, CHUNK, LANES), F32),
                        pltpu.VMEM((2, CHUNK, CHUNK), BF16), pltpu.VMEM((2 * N_GLA_MASKS, CHUNK, CHUNK), F32)],
        compiler_params=_cparams("parallel", "parallel"),
        name="hgrn2",
    )(*args)
    return (res[0], res[1]) if with_out else (None, res[0])


def _softplus(x):
    return jnp.maximum(x, 0.0) + jnp.log1p(jnp.exp(-jnp.abs(x)))


LOG2E = math.log2(math.e)


def _ssd_fill_masks(tri_ref, mask_ref):
    ti = lax.broadcasted_iota(jnp.int32, (CHUNK, CHUNK), 0)
    si = lax.broadcasted_iota(jnp.int32, (CHUNK, CHUNK), 1)
    for d in (0, 1):
        incl = jnp.where((ti <= si) if d else (ti >= si), 1.0, 0.0)
        mask_ref[d] = incl
        tri_ref[d] = incl.astype(BF16)


def _ssd_chunk(xs, bc, cc, cb, dt_col_raw, dt_row_raw, bias_row, bias_col, alog_row, alog_col,
               sn_ref, tri, rev, with_out):
    t = CHUNK
    hp = xs.shape[1]
    npair = hp // LANES
    lane = lax.broadcasted_iota(jnp.int32, (t, LANES), 1)
    first = lane < SSM_HEADDIM

    dt_col = _softplus(dt_col_raw + bias_row)
    a_col = dt_col * (-LOG2E * jnp.exp(alog_row))
    hi, mid, lo = _split3(a_col)
    acum_col = _dot(tri, hi) + _dot(tri, mid) + _dot(tri, lo)
    if with_out:
        dt_row = _softplus(dt_row_raw + bias_col)
        a_row = dt_row * (-LOG2E * jnp.exp(alog_col))
        hi, mid, lo = _split3(a_row)
        acum_row = _dot_nt(hi, tri) + _dot_nt(mid, tri) + _dot_nt(lo, tri)

    def pair_cols(m, i):
        return jnp.where(first, m[:, 2 * i:2 * i + 1], m[:, 2 * i + 1:2 * i + 2])

    sn = sn_ref[...]
    ys, xws, dec = [], [], []
    for i in range(npair):
        dt_p = pair_cols(dt_col, i)
        ac_p = pair_cols(acum_col, i)
        xdt = xs[:, i * LANES:(i + 1) * LANES].astype(F32) * dt_p
        atot = ac_p[0:1, :] if rev else ac_p[t - 1:t, :]
        xws.append((xdt * jnp.exp2(atot - ac_p)).astype(BF16))
        dec.append(jnp.exp2(atot))
        if with_out:
            ws = []
            for j in (2 * i, 2 * i + 1):
                diff = jnp.minimum(acum_col[:, j:j + 1] - acum_row[j:j + 1, :], 0.0)
                ws.append((cb * jnp.exp2(diff)).astype(BF16))
            lhs = jnp.concatenate(ws, axis=1)
            zero = jnp.zeros_like(xdt)
            rhs = jnp.concatenate([jnp.where(first, xdt, zero), jnp.where(first, zero, xdt)],
                                  axis=0).astype(BF16)
            y_state = _dot(cc, sn[:, i * LANES:(i + 1) * LANES].astype(BF16)) * jnp.exp2(ac_p)
            ys.append(_dot(lhs, rhs) + y_state)
    xw = jnp.concatenate(xws, axis=1)
    sn_ref[...] = sn * jnp.concatenate(dec, axis=1) + _dot_tn(bc, xw)
    return jnp.concatenate(ys, axis=1) if with_out else None


SSD_UNROLL = 2


def _ssd_kernel(*refs, nc, with_out):
    if with_out:
        (xs_ref, b_ref, c_ref, dtc_ref, dtr_ref, biasr_ref, biasc_ref, alogr_ref, alogc_ref,
         dskip_ref, s0_ref, y_ref, st_out_ref, sn_ref, tri_ref, mask_ref) = refs
    else:
        (xs_ref, b_ref, dtc_ref, biasr_ref, alogr_ref, s0_ref, st_out_ref, sn_ref, tri_ref, mask_ref) = refs
        c_ref = dtr_ref = biasc_ref = alogc_ref = dskip_ref = y_ref = None
    _ssd_fill_masks(tri_ref, mask_ref)
    sn_ref[...] = s0_ref[...]
    unroll = math.gcd(nc, SSD_UNROLL)
    if with_out:
        dsk = dskip_ref[...]

        def init(ci, carry):
            rows = pl.ds(pl.multiple_of(ci * CHUNK, CHUNK), CHUNK)
            y_ref[rows, :] = xs_ref[rows, :].astype(F32) * dsk
            return carry

        lax.fori_loop(0, nc, init, 0)

    def body(ci, carry):
        for u in range(unroll):
            for d in (0, 1):
                step = ci * unroll + u
                cidx = step if d == 0 else nc - 1 - step
                rows = pl.ds(pl.multiple_of(cidx * CHUNK, CHUNK), CHUNK)
                xs = xs_ref[rows, :]
                bc = b_ref[rows, :]
                if with_out:
                    cc = c_ref[rows, :]
                    cb = _dot_nt(cc, bc) * mask_ref[d]
                    y = _ssd_chunk(xs, bc, cc, cb, dtc_ref[d, rows, :], dtr_ref[d, :, rows],
                                   biasr_ref[d], biasc_ref[d], alogr_ref[d], alogc_ref[d],
                                   sn_ref.at[d], tri_ref[d], d == 1, True)
                    y_ref[rows, :] += y
                else:
                    _ssd_chunk(xs, bc, None, None, dtc_ref[d, rows, :], None, biasr_ref[d], None,
                               alogr_ref[d], None, sn_ref.at[d], tri_ref[d], d == 1, False)
        return carry

    lax.fori_loop(0, nc // unroll, body, 0)
    st_out_ref[...] = sn_ref[...]


def _ssd_call(xbc, dt_col, dt_row, bias, alog, dskip, s0, with_out=True):
    bn, l, _ = xbc.shape
    g = SSM_GROUPS
    hg = SSM_HEADS // g
    hp = hg * SSM_HEADDIM
    d_inner = SSM_HEADS * SSM_HEADDIM
    nb_off = d_inner // LANES
    xs_spec = pl.BlockSpec((None, l, hp), lambda b, gi: (b, 0, gi))
    b_spec = pl.BlockSpec((None, l, LANES), lambda b, gi: (b, 0, nb_off + gi))
    c_spec = pl.BlockSpec((None, l, LANES), lambda b, gi: (b, 0, nb_off + g + gi))
    dtc_spec = pl.BlockSpec((None, None, 2, l, hg), lambda b, gi: (b, gi, 0, 0, 0))
    dtr_spec = pl.BlockSpec((None, None, 2, hg, l), lambda b, gi: (b, gi, 0, 0, 0))
    prow = pl.BlockSpec((None, 2, 1, hg), lambda b, gi: (gi, 0, 0, 0))
    pcol = pl.BlockSpec((None, 2, hg, 1), lambda b, gi: (gi, 0, 0, 0))
    dsk_spec = pl.BlockSpec((None, 1, hp), lambda b, gi: (gi, 0, 0))
    st_spec = pl.BlockSpec((None, None, 2, SSM_STATE, hp), lambda b, gi: (b, gi, 0, 0, 0))
    bias_r = bias.reshape(g, 2, 1, hg)
    bias_c = bias.reshape(g, 2, hg, 1)
    alog_r = alog.reshape(g, 2, 1, hg)
    alog_c = alog.reshape(g, 2, hg, 1)
    if with_out:
        in_specs = [xs_spec, b_spec, c_spec, dtc_spec, dtr_spec, prow, pcol, prow, pcol, dsk_spec, st_spec]
        args = [xbc, xbc, xbc, dt_col, dt_row, bias_r, bias_c, alog_r, alog_c, dskip, s0]
        out_specs = [xs_spec, st_spec]
        out_shape = [jax.ShapeDtypeStruct((bn, l, d_inner), F32), jax.ShapeDtypeStruct(s0.shape, F32)]
    else:
        in_specs = [xs_spec, b_spec, dtc_spec, prow, prow, st_spec]
        args = [xbc, xbc, dt_col, bias_r, alog_r, s0]
        out_specs = [st_spec]
        out_shape = [jax.ShapeDtypeStruct(s0.shape, F32)]
    res = pl.pallas_call(
        functools.partial(_ssd_kernel, nc=l // CHUNK, with_out=with_out),
        grid=(bn, g),
        in_specs=in_specs,
        out_specs=out_specs,
        out_shape=out_shape,
        scratch_shapes=[pltpu.VMEM((2, SSM_STATE, hp), F32), pltpu.VMEM((2, CHUNK, CHUNK), BF16),
                        pltpu.VMEM((2, CHUNK, CHUNK), F32)],
        compiler_params=_cparams("parallel", "parallel"),
        name="ssd",
    )(*args)
    return (res[0], res[1]) if with_out else (None, res[0])


def _merge_kernel(x_ref, oatt_ref, ohg_ref, hgate_ref, y_ref, z_ref, gates_ref, hgn_ref, ssn_ref,
                  gm_ref, wba_ref, wbh_ref, wbs_ref, wout_ref, o_ref):
    d = x_ref.shape[1]
    hgn = hgn_ref[...]
    parts = []
    for s in range(ohg_ref.shape[1] // LANES):
        sl = slice(s * LANES, (s + 1) * LANES)
        o = ohg_ref[:, sl]
        o = o * lax.rsqrt(jnp.mean(o * o, axis=1, keepdims=True) + EPS) * hgn
        parts.append((o * _silu(hgate_ref[:, sl].astype(F32))).astype(BF16))
    o_hg = jnp.concatenate(parts, axis=1)
    ys = y_ref[...] * _silu(z_ref[...].astype(F32))
    o_ssm = (ys * lax.rsqrt(jnp.mean(ys * ys, axis=1, keepdims=True) + EPS) * ssn_ref[...]).astype(BF16)
    g_att = _sigmoid(gates_ref[:, 0:d].astype(F32))
    g_hg = _sigmoid(gates_ref[:, d:2 * d].astype(F32))
    g_ssm = _sigmoid(gates_ref[:, 2 * d:3 * d].astype(F32))
    m = (g_att * _dot(oatt_ref[...], wba_ref[...]) + g_hg * _dot(o_hg, wbh_ref[...])
         + g_ssm * _dot(o_ssm, wbs_ref[...]))
    mix = _dot(m.astype(BF16), wout_ref[...])
    o_ref[...] = x_ref[...] + gm_ref[...] * mix


def _merge_call(x, o_att, o_hg, y, gz, hg_norm_g, ssm_norm_g, g_m, wba, wbh, wbs, wout):
    bn, l, d = x.shape
    tm = min(l, DENSE_TM)
    di = y.shape[2]
    assert di == 2 * d and gz.shape[2] == 6 * d
    tok = lambda w, j=0: pl.BlockSpec((None, tm, w), lambda b, i: (b, i, j))
    full = lambda a: pl.BlockSpec(a.shape, lambda b, i: (0,) * a.ndim, pipeline_mode=pl.Buffered(1))
    hgn = hg_norm_g.reshape(1, LANES)
    ssn = ssm_norm_g.reshape(1, di)
    return pl.pallas_call(
        _merge_kernel,
        grid=(bn, l // tm),
        in_specs=[tok(d), tok(d), tok(d), tok(d, 2), tok(di), tok(di, 0), tok(N_BRANCH * d, 1), full(hgn), full(ssn),
                  pl.BlockSpec((None, 1, d), lambda b, i: (b, 0, 0)),
                  full(wba), full(wbh), full(wbs), full(wout)],
        out_specs=tok(d),
        out_shape=jax.ShapeDtypeStruct((bn, l, d), F32),
        compiler_params=_cparams("parallel", "parallel"),
        name="merge",
    )(x, o_att, o_hg, gz, y, gz, gz, hgn, ssn, g_m.reshape(bn, 1, d), wba, wbh, wbs, wout)


def _ffn_kernel(x_ref, g_ref, sh_ref, sc_ref, gf_ref, w1_ref, w3_ref, w2_ref, fin_ref, o_ref, *, final):
    x = x_ref[...]
    h = x * lax.rsqrt(jnp.mean(x * x, axis=-1, keepdims=True) + EPS) * g_ref[...]
    h = (h * (1.0 + sc_ref[...]) + sh_ref[...]).astype(BF16)
    u = _dot(h, w1_ref[...])
    t = _dot(h, w3_ref[...])
    a = (_silu(u) * t).astype(BF16)
    y = x + gf_ref[...] * _dot(a, w2_ref[...])
    if final:
        y = y * lax.rsqrt(jnp.mean(y * y, axis=-1, keepdims=True) + EPS) * fin_ref[...]
    o_ref[...] = y


def _ffn_call(x, g, shift, scale, gate, w1, w3, w2, final_g, final):
    bn, l, d = x.shape
    tm = min(l, DENSE_TM)
    tok = pl.BlockSpec((None, tm, d), lambda b, i: (b, i, 0))
    vec = pl.BlockSpec((None, 1, d), lambda b, i: (b, 0, 0))
    row = pl.BlockSpec((1, d), lambda b, i: (0, 0))
    full = lambda a: pl.BlockSpec(a.shape, lambda b, i: (0, 0), pipeline_mode=pl.Buffered(1))
    return pl.pallas_call(
        functools.partial(_ffn_kernel, final=final),
        grid=(bn, l // tm),
        in_specs=[tok, row, vec, vec, vec, full(w1), full(w3), full(w2), row],
        out_specs=tok,
        out_shape=jax.ShapeDtypeStruct((bn, l, d), F32),
        compiler_params=_cparams("parallel", "parallel"),
        name="ffn",
    )(x, g.reshape(1, d), shift.reshape(bn, 1, d), scale.reshape(bn, 1, d), gate.reshape(bn, 1, d),
      w1, w3, w2, final_g.reshape(1, d))


def _rope_perm():
    idx = []
    for lane in range(LANES):
        half, mp, axis, i = lane // 64, (lane // 32) % 2, (lane // 16) % 2, lane % 16
        idx.append(mp * 64 + axis * 32 + half * 16 + i)
    return idx


def _rope_tables(l):
    rows = l // GRID_W
    row = jnp.repeat(jnp.arange(rows, dtype=F32), GRID_W)
    col = jnp.broadcast_to(jnp.arange(GRID_W, dtype=F32), (rows, GRID_W)).reshape(-1)
    nfreq = ATT_HD // 4
    freqs = ROPE_THETA ** (-jnp.arange(nfreq, dtype=F32) / nfreq)
    ang_r = row[:, None] * freqs[None, :]
    ang_c = col[:, None] * freqs[None, :]
    ang = jnp.concatenate([ang_r, ang_c, ang_r, ang_c], axis=1)
    ang = jnp.concatenate([ang, ang], axis=1)
    sign = jnp.where(jnp.arange(LANES) < LANES // 2, -1.0, 1.0).astype(F32)
    return jnp.cos(ang), jnp.sin(ang) * sign[None, :]


def _dt_layouts(dt_raw, bn, l):
    hg = SSM_HEADS // SSM_GROUPS
    t = dt_raw.reshape(bn, l, 2, SSM_GROUPS, hg)
    return t.transpose(0, 3, 2, 1, 4), t.transpose(0, 3, 2, 4, 1)


def kernel(x, c, ctx, c_ctx, w_ada, b_ada, norm1_g, w_in, att_lambda, att_norm_g, hg_lb_logits, hg_norm_g,
           ssm_conv_w, ssm_conv_b, ssm_dt_bias, ssm_a_log, ssm_d, ssm_norm_g, w_branch_att, w_branch_hg,
           w_branch_ssm, w_out, norm2_g, ffn_w1, ffn_w3, ffn_w2, final_g):
    bn, l, d = x.shape
    lc = ctx.shape[1]
    depth = w_ada.shape[0]
    att_w = ATT_HEADS * 2 * ATT_HD
    hg_w = HG_HEADS * LANES
    d_inner = SSM_HEADS * SSM_HEADDIM
    xbc_w = d_inner + 2 * SSM_GROUPS * SSM_STATE
    hgrp = SSM_HEADS // SSM_GROUPS

    sizes = (att_w, att_w, hg_w, hg_w, hg_w, xbc_w, SSM_HEADS, SSM_HEADS, att_w, hg_w, hg_w, d_inner, N_BRANCH * d)
    offs = [0]
    for s in sizes:
        offs.append(offs[-1] + s)
    (o_ak, o_av, o_hff, o_hfb, o_hi, o_xbc, o_dtf, o_dtb, o_aq, o_hq, o_hgate, o_z, o_gates, _) = offs

    perm = jnp.asarray([h * LANES + p for h in range(ATT_HEADS) for p in _rope_perm()], jnp.int32)
    cos_t, sin_t = _rope_tables(l)
    q_scale = ATT_HD ** -0.5 * math.log2(math.e)

    lb_w = jax.nn.softmax(hg_lb_logits.astype(F32), axis=1)
    lb = jnp.cumsum(lb_w, axis=1) - lb_w[:, :1]

    pad_rows = (-(bn + 1)) % SUBLANES
    c_rows = jnp.concatenate([c, c_ctx[None, :], jnp.zeros((pad_rows, d), F32)], axis=0)
    mod_all = _mod_call(c_rows, w_ada, b_ada)

    xc = ctx
    for li in range(depth):
        last = li == depth - 1
        mod = mod_all[li, :bn]
        mod_c = jnp.broadcast_to(mod_all[li, bn:bn + 1], (bn, 6 * d))
        sh_m, sc_m, g_m, sh_f, sc_f, g_f = [mod[:, i * d:(i + 1) * d] for i in range(6)]
        shc_m, scc_m, gc_m, shc_f, scc_f, gc_f = [mod_c[:, i * d:(i + 1) * d] for i in range(6)]

        lam_init = 0.8 - 0.6 * math.exp(-0.3 * li)
        al = att_lambda[li].astype(F32)
        lam = (jnp.exp(jnp.sum(al[0] * al[1])) - jnp.exp(jnp.sum(al[2] * al[3])) + lam_init).reshape(1)

        w = w_in[li]
        wb = lambda o, n: w[:, o:o + n].astype(BF16)
        w_q = wb(o_aq, att_w)[:, perm]
        w_k = wb(o_ak, att_w)[:, perm]
        w_vt = wb(o_av, att_w).T
        w_hi = wb(o_hi, hg_w)
        w_f = jnp.concatenate([wb(o_hff, 2 * hg_w), wb(o_dtf, 2 * SSM_HEADS),
                               jnp.zeros((d, LANES - 2 * SSM_HEADS), BF16)], axis=1)
        w_hq = wb(o_hq, hg_w)
        w_xbc = wb(o_xbc, xbc_w)
        w_gz = jnp.concatenate([wb(o_z, d_inner), wb(o_hgate, hg_w), wb(o_gates, N_BRANCH * d)], axis=1)

        lb_f = lb[0, li].reshape(1, hg_w)
        lb_b = lb[1, li].reshape(1, hg_w)
        bias_g = ssm_dt_bias[li].reshape(2, SSM_GROUPS, hgrp).transpose(1, 0, 2)
        alog_g = ssm_a_log[li].astype(F32).reshape(2, SSM_GROUPS, hgrp).transpose(1, 0, 2)
        dskip = jnp.repeat(ssm_d[li], SSM_HEADDIM).reshape(SSM_GROUPS, 1, hgrp * SSM_HEADDIM)
        wba = w_branch_att[li].astype(BF16)
        wbh = w_branch_hg[li].astype(BF16)
        wbs = w_branch_ssm[li].astype(BF16)
        wo = w_out[li].astype(BF16)
        w1 = ffn_w1[li].astype(BF16)
        w3 = ffn_w3[li].astype(BF16)
        w2 = ffn_w2[li].astype(BF16)

        hc = _adaln_call(xc, norm1_g[li], shc_m, scc_m)
        kc = _proj_call(hc, w_k, BF16, name="proj_kc")
        vtc = _projt_call(hc, w_vt, BF16, name="proj_vtc")
        vhc = _proj_call(hc, w_hi, BF16, name="proj_vhc")
        frc = _proj_call(hc, w_f, F32, tn=w_f.shape[1], name="proj_fc")
        xbcc = _conv_call(_proj_call(hc, w_xbc, BF16, name="proj_xbcc"), ssm_conv_w[li], ssm_conv_b[li])
        dtc = frc[:, :, 2 * hg_w:2 * hg_w + 2 * SSM_HEADS]
        dtc_col, dtc_row = _dt_layouts(dtc, bn, lc)
        zero_g = jnp.zeros((bn, HG_HEADS, 2, LANES, LANES), F32)
        zero_s = jnp.zeros((bn, SSM_GROUPS, 2, SSM_STATE, hgrp * SSM_HEADDIM), F32)
        if last:
            _, sg_ctx = _gla_call(None, frc, vhc, 0, lb_f, lb_b, zero_g, with_out=False)
            _, ss_ctx = _ssd_call(xbcc, dtc_col, dtc_row, bias_g, alog_g, dskip, zero_s, with_out=False)
        else:
            qc = _proj_call(hc, w_q, BF16, epi="scale", scale=q_scale, name="proj_qc")
            hqc = _proj_call(hc, w_hq, BF16, epi="silu", name="proj_hqc")
            gzc = _proj_call(hc, w_gz, BF16, name="proj_gzc")
            oc_hg, sg_ctx = _gla_call(hqc, frc, vhc, 0, lb_f, lb_b, zero_g)
            yc, ss_ctx = _ssd_call(xbcc, dtc_col, dtc_row, bias_g, alog_g, dskip, zero_s)
            oc_att = _attn_call(lam, qc, [(kc, vtc)], att_norm_g[li], 1.0 - lam_init)

        h = _adaln_call(x, norm1_g[li], sh_m, sc_m)
        q = _proj_call(h, w_q, BF16, epi="rope", tables=(cos_t * q_scale, sin_t * q_scale), name="proj_q")
        k = _proj_call(h, w_k, BF16, epi="rope", tables=(cos_t, sin_t), name="proj_k")
        vt = _projt_call(h, w_vt, BF16, name="proj_vt")
        vh = _proj_call(h, w_hi, BF16, name="proj_vh")
        fr = _proj_call(h, w_f, F32, tn=w_f.shape[1], name="proj_f")
        hq = _proj_call(h, w_hq, BF16, epi="silu", name="proj_hq")
        xbc = _conv_call(_proj_call(h, w_xbc, BF16, name="proj_xbc"), ssm_conv_w[li], ssm_conv_b[li])
        dt = fr[:, :, 2 * hg_w:2 * hg_w + 2 * SSM_HEADS]
        gz = _proj_call(h, w_gz, BF16, name="proj_gz")
        dt_col, dt_row = _dt_layouts(dt, bn, l)

        o_att = _attn_call(lam, q, [(kc, vtc), (k, vt)], att_norm_g[li], 1.0 - lam_init)
        o_hg, _ = _gla_call(hq, fr, vh, 0, lb_f, lb_b, sg_ctx)
        y, _ = _ssd_call(xbc, dt_col, dt_row, bias_g, alog_g, dskip, ss_ctx)

        x = _merge_call(x, o_att, o_hg, y, gz, hg_norm_g[li], ssm_norm_g[li], g_m, wba, wbh, wbs, wo)
        x = _ffn_call(x, norm2_g[li], sh_f, sc_f, g_f, w1, w3, w2, final_g, final=last)

        if not last:
            xc = _merge_call(xc, oc_att, oc_hg, yc, gzc, hg_norm_g[li], ssm_norm_g[li], gc_m,
                             wba, wbh, wbs, wo)
            xc = _ffn_call(xc, norm2_g[li], shc_f, scc_f, gc_f, w1, w3, w2, final_g, final=False)
    return x
```

```python
import functools
import math

import jax
import jax.numpy as jnp
from jax import lax
from jax.experimental import pallas as pl
from jax.experimental.pallas import tpu as pltpu

F32 = jnp.float32
BF16 = jnp.bfloat16

EPS = 1e-6
GRID_W = 64
ROPE_THETA = 10000.0
ATT_HEADS = 8
ATT_HD = 64
HG_HEADS = 8
SSM_HEADS = 32
SSM_HEADDIM = 64
SSM_GROUPS = 4
SSM_STATE = 128
SSM_CONV = 5
N_BRANCH = 3

LANES = 128
SUBLANES = 8
VMEM_LIMIT = 56 * 1024 * 1024

DENSE_TM = 512
CHUNK = 128
DIAG = 8
LEVELS = (64, 32, 16, 8)


def _cparams(*sem):
    return pltpu.CompilerParams(dimension_semantics=sem, vmem_limit_bytes=VMEM_LIMIT)


def _sigmoid(x):
    return 0.5 * jnp.tanh(0.5 * x) + 0.5


def _silu(x):
    hx = 0.5 * x
    return hx * jnp.tanh(hx) + hx


def _split2(x):
    hi = x.astype(BF16)
    lo = (x - hi.astype(F32)).astype(BF16)
    return hi, lo


def _dot(a, b):
    return jnp.dot(a, b, preferred_element_type=F32)


def _dot_nt(a, b):
    return lax.dot_general(a, b, (((1,), (1,)), ((), ())), preferred_element_type=F32)


def _dot_tn(a, b):
    return lax.dot_general(a, b, (((0,), (0,)), ((), ())), preferred_element_type=F32)


def _mod_kernel(c_ref, w_ref, b_ref, o_ref):
    a = _silu(c_ref[...])
    a_hi = a.astype(BF16)
    a_lo = (a - a_hi.astype(F32)).astype(BF16)
    w = w_ref[...]
    w_hi = w.astype(BF16)
    w_lo = (w - w_hi.astype(F32)).astype(BF16)
    o_ref[...] = _dot(a_hi, w_hi) + _dot(a_lo, w_hi) + _dot(a_hi, w_lo) + b_ref[...]


def _mod_call(c_rows, w_ada, b_ada):
    depth, d, n = w_ada.shape
    r = c_rows.shape[0]
    tn = 1536
    return pl.pallas_call(
        _mod_kernel,
        grid=(depth, n // tn),
        in_specs=[
            pl.BlockSpec((r, d), lambda l, j: (0, 0)),
            pl.BlockSpec((None, d, tn), lambda l, j: (l, 0, j)),
            pl.BlockSpec((None, 1, tn), lambda l, j: (l, 0, j)),
        ],
        out_specs=pl.BlockSpec((None, r, tn), lambda l, j: (l, 0, j)),
        out_shape=jax.ShapeDtypeStruct((depth, r, n), F32),
        compiler_params=_cparams("parallel", "parallel"),
        name="mod",
    )(c_rows, w_ada, b_ada.reshape(depth, 1, n))


def _adaln_kernel(x_ref, g_ref, sh_ref, sc_ref, o_ref):
    x = x_ref[...]
    y = x * lax.rsqrt(jnp.mean(x * x, axis=-1, keepdims=True) + EPS) * g_ref[...]
    o_ref[...] = (y * (1.0 + sc_ref[...]) + sh_ref[...]).astype(o_ref.dtype)


def _adaln_call(x, g, shift, scale):
    bn, l, d = x.shape
    tl = min(l, 512)
    vec = pl.BlockSpec((None, 1, d), lambda b, i: (b, 0, 0))
    return pl.pallas_call(
        _adaln_kernel,
        grid=(bn, l // tl),
        in_specs=[
            pl.BlockSpec((None, tl, d), lambda b, i: (b, i, 0)),
            pl.BlockSpec((1, d), lambda b, i: (0, 0)),
            vec, vec,
        ],
        out_specs=pl.BlockSpec((None, tl, d), lambda b, i: (b, i, 0)),
        out_shape=jax.ShapeDtypeStruct((bn, l, d), BF16),
        compiler_params=_cparams("parallel", "parallel"),
        name="adaln",
    )(x, g.reshape(1, d), shift.reshape(bn, 1, d), scale.reshape(bn, 1, d))


def _proj_kernel(*refs, epi, scale):
    if epi == "rope":
        h_ref, w_ref, cos_ref, sin_ref, o_ref = refs
    else:
        h_ref, w_ref, o_ref = refs
    acc = _dot(h_ref[...], w_ref[...])
    if epi == "rope":
        cos = cos_ref[...]
        sin = sin_ref[...]
        for s in range(acc.shape[1] // LANES):
            t = acc[:, s * LANES:(s + 1) * LANES]
            r = t * cos + pltpu.roll(t, LANES // 2, 1) * sin
            o_ref[:, s * LANES:(s + 1) * LANES] = r.astype(o_ref.dtype)
    elif epi == "silu":
        o_ref[...] = _silu(acc).astype(o_ref.dtype)
    elif epi == "scale":
        o_ref[...] = (acc * scale).astype(o_ref.dtype)
    else:
        o_ref[...] = acc.astype(o_ref.dtype)


def _projt_kernel(h_ref, wt_ref, o_ref):
    o_ref[...] = _dot_nt(wt_ref[...], h_ref[...]).astype(o_ref.dtype)


def _projt_call(h, wt, out_dtype, name="projt"):
    bn, l, k = h.shape
    n = wt.shape[0]
    tm = min(l, 1024)
    tn = min(n, 1024)
    return pl.pallas_call(
        _projt_kernel,
        grid=(bn, l // tm, n // tn),
        in_specs=[
            pl.BlockSpec((None, tm, k), lambda b, i, j: (b, i, 0)),
            pl.BlockSpec((tn, k), lambda b, i, j: (j, 0)),
        ],
        out_specs=pl.BlockSpec((None, tn, tm), lambda b, i, j: (b, j, i)),
        out_shape=jax.ShapeDtypeStruct((bn, n, l), out_dtype),
        compiler_params=_cparams("parallel", "parallel", "parallel"),
        name=name,
    )(h, wt)


def _proj_call(h, w, out_dtype, epi="none", tables=None, scale=1.0, tn=None, name="proj"):
    bn, l, k = h.shape
    n = w.shape[1]
    tn = min(n, 1024) if tn is None else tn
    tm = min(l, 1024 if tn <= 1024 else 512)
    assert n % tn == 0 and l % tm == 0
    nl = l // tm
    in_specs = [
        pl.BlockSpec((None, tm, k), lambda b, i, j: (b, i, 0)),
        pl.BlockSpec((k, tn), lambda b, i, j: (0, j)),
    ]
    args = [h, w]
    if epi == "rope":
        tab = pl.BlockSpec((tm, LANES), lambda b, i, j: (i, 0))
        in_specs += [tab, tab]
        args += list(tables)
    return pl.pallas_call(
        functools.partial(_proj_kernel, epi=epi, scale=scale),
        grid=(bn, nl, n // tn),
        in_specs=in_specs,
        out_specs=pl.BlockSpec((None, tm, tn), lambda b, i, j: (b, i, j)),
        out_shape=jax.ShapeDtypeStruct((bn, l, n), out_dtype),
        compiler_params=_cparams("parallel", "parallel", "parallel"),
        name=name,
    )(*args)


def _conv_kernel(u_ref, w_ref, b_ref, o_ref, pad_ref, *, rows):
    l, cb = u_ref.shape
    halo = SUBLANES
    pad_ref[0:halo, :] = jnp.zeros((halo, cb), F32)
    pad_ref[halo + l:2 * halo + l, :] = jnp.zeros((halo, cb), F32)
    for r0 in range(0, l, rows):
        pad_ref[halo + r0:halo + r0 + rows, :] = u_ref[r0:r0 + rows, :].astype(F32)
    w = w_ref[...]
    bias = b_ref[...]
    half = SSM_CONV // 2
    n = rows + 2 * halo
    for r0 in range(0, l, rows):
        xa = pad_ref[r0:r0 + n, :]
        acc = bias + w[half:half + 1, :] * xa[halo:halo + rows, :]
        for k in range(SSM_CONV):
            if k == half:
                continue
            shifted = pltpu.roll(xa, n - halo - (k - half), 0)[:rows, :]
            acc = acc + w[k:k + 1, :] * shifted
        o_ref[r0:r0 + rows, :] = _silu(acc).astype(o_ref.dtype)


def _conv_call(u, w, b):
    bn, l, c = u.shape
    cb = 512
    rows = min(l, 512)
    return pl.pallas_call(
        functools.partial(_conv_kernel, rows=rows),
        grid=(bn, c // cb),
        in_specs=[
            pl.BlockSpec((None, l, cb), lambda b_, j: (b_, 0, j)),
            pl.BlockSpec((SSM_CONV, cb), lambda b_, j: (0, j)),
            pl.BlockSpec((1, cb), lambda b_, j: (0, j)),
        ],
        out_specs=pl.BlockSpec((None, l, cb), lambda b_, j: (b_, 0, j)),
        out_shape=jax.ShapeDtypeStruct((bn, l, c), BF16),
        scratch_shapes=[pltpu.VMEM((l + 2 * SUBLANES, cb), F32)],
        compiler_params=_cparams("parallel", "parallel"),
        name="conv",
    )(u, w, b.reshape(1, c))


ATT_TQ = 256
ATT_BLOCK_Q = 2048


def _attn_kernel(*refs, nsrc, post_scale):
    lam_ref, q_ref = refs[0], refs[1]
    k_refs = [refs[2 + 2 * i] for i in range(nsrc)]
    vt_refs = [refs[3 + 2 * i] for i in range(nsrc)]
    g_ref, o_ref = refs[2 + 2 * nsrc:4 + 2 * nsrc]
    s_scr = refs[4 + 2 * nsrc:]
    tq = ATT_TQ if q_ref.shape[0] % ATT_TQ == 0 else q_ref.shape[0]
    nt = q_ref.shape[0] // tq
    lane = lax.broadcasted_iota(jnp.int32, (tq, LANES), 1)
    second_map = ((lane // 32) % 2) == 1
    zero = jnp.zeros((tq, LANES), BF16)
    gain = g_ref[...] * post_scale
    lam = lam_ref[0]

    def key_chunks(sizes):
        out, off = [], 0
        for i in range(nsrc):
            lk = k_refs[i].shape[0]
            kc = next(s for s in sizes if lk % s == 0)
            out += [(i, c0, kc, off + c0) for c0 in range(0, lk, kc)]
            off += lk
        return out

    chunks = key_chunks((256, LANES))

    def stacked_q(t):
        q = q_ref[t * tq:(t + 1) * tq, :]
        return jnp.concatenate([jnp.where(second_map, zero, q), jnp.where(second_map, q, zero)], axis=0)

    def scores(qq, slot, chunk, mx):
        i, c0, kc, off = chunk
        s = _dot_nt(k_refs[i][c0:c0 + kc, :], qq)
        s_scr[slot][off:off + kc, :] = s
        cm = jnp.max(s.reshape(kc // SUBLANES, SUBLANES, 2 * tq), axis=0)
        return cm if mx is None else jnp.maximum(mx, cm)

    ones_rows = 2 * SUBLANES

    def probs_pv(slot, chunk, m, acc):
        i, c0, kc, off = chunk
        p = jnp.exp2(s_scr[slot][off:off + kc, :] - m).astype(BF16)
        vt1 = jnp.concatenate([vt_refs[i][:, c0:c0 + kc], jnp.ones((ones_rows, kc), BF16)], axis=0)
        part = _dot(vt1, p)
        return part if acc is None else acc + part

    qq = stacked_q(0)
    mx = None
    for chunk in chunks:
        mx = scores(qq, 0, chunk, mx)
    for t in range(nt):
        slot = t % 2
        m = jnp.max(mx, axis=0, keepdims=True)
        mx, acc = None, None
        if t + 1 < nt:
            qq = stacked_q(t + 1)
            for chunk in chunks:
                mx = scores(qq, 1 - slot, chunk, mx)
        for chunk in chunks:
            acc = probs_pv(slot, chunk, m, acc)
        den = acc[LANES:LANES + 1, :]
        c0 = 1.0 / den[:, :tq]
        c1 = lam / den[:, tq:]
        o = (acc[:LANES, :tq] * c0 - acc[:LANES, tq:] * c1).T
        o = o * lax.rsqrt(jnp.mean(o * o, axis=1, keepdims=True) + EPS) * gain
        o_ref[t * tq:(t + 1) * tq, :] = o.astype(o_ref.dtype)


def _attn_call(lam, q, kvs, g, post_scale):
    bn, lq, hw = q.shape
    nh = hw // LANES
    tq = min(lq, ATT_BLOCK_Q)
    in_specs = [
        pl.BlockSpec(memory_space=pltpu.SMEM),
        pl.BlockSpec((None, tq, LANES), lambda b, h, i: (b, i, h)),
    ]
    args = [lam, q]
    for k, vt in kvs:
        lk = k.shape[1]
        in_specs += [pl.BlockSpec((None, lk, LANES), lambda b, h, i: (b, 0, h)),
                     pl.BlockSpec((None, LANES, lk), lambda b, h, i: (b, h, 0))]
        args += [k, vt]
    in_specs.append(pl.BlockSpec((1, LANES), lambda b, h, i: (0, 0)))
    args.append(g.reshape(1, LANES))
    lk_total = sum(k.shape[1] for k, _ in kvs)
    sub = ATT_TQ if tq % ATT_TQ == 0 else tq
    return pl.pallas_call(
        functools.partial(_attn_kernel, nsrc=len(kvs), post_scale=post_scale),
        grid=(bn, nh, lq // tq),
        in_specs=in_specs,
        out_specs=pl.BlockSpec((None, tq, LANES), lambda b, h, i: (b, i, h)),
        out_shape=jax.ShapeDtypeStruct((bn, lq, hw), BF16),
        scratch_shapes=[pltpu.VMEM((lk_total, 2 * sub), F32), pltpu.VMEM((lk_total, 2 * sub), F32)],
        compiler_params=_cparams("parallel", "parallel", "parallel"),
        name="diff_attn",
    )(*args)


N_GLA_MASKS = len(LEVELS) + 1


def _gla_fill_masks(tri_ref, mask_ref):
    ti = lax.broadcasted_iota(jnp.int32, (CHUNK, CHUNK), 0)
    si = lax.broadcasted_iota(jnp.int32, (CHUNK, CHUNK), 1)
    x = ti ^ si
    for d in (0, 1):
        strict = (ti < si) if d else (ti > si)
        incl = (ti <= si) if d else (ti >= si)
        tri_ref[d] = jnp.where(incl, 1.0, 0.0).astype(BF16)
        for li, h in enumerate(LEVELS):
            mask_ref[d * N_GLA_MASKS + li] = jnp.where(strict & (x >= h) & (x < 2 * h), 1.0, 0.0)
        mask_ref[d * N_GLA_MASKS + len(LEVELS)] = jnp.where(incl & (x < DIAG), 1.0, 0.0)


def _gla_chunk(q, fr, v, lb, st_ref, b_ref, tri_ref, mask_ref, rev, with_out):
    c = CHUNK
    mbase = N_GLA_MASKS if rev else 0
    tri = tri_ref[1 if rev else 0]
    e = jnp.exp(-jnp.abs(fr))
    r = 1.0 / (1.0 + e)
    pos = fr >= 0.0
    sig = jnp.where(pos, r, e * r)
    nsig = jnp.where(pos, e * r, r)
    lf = jnp.log2(lb + (1.0 - lb) * sig)
    k = (1.0 - lb) * nsig
    hi, lo = _split2(lf)
    b = _dot(tri, hi) + _dot(tri, lo)
    b_ref[...] = b
    b_tot = b_ref[0:1, :] if rev else b_ref[c - 1:c, :]
    st = st_ref[...]
    out = None
    if with_out:
        qf = q.astype(F32)
        row = lax.broadcasted_iota(jnp.int32, (c, LANES), 0)
        a = None
        for li, h in enumerate(LEVELS):
            pieces = []
            for base in range(0, c, 2 * h):
                rr = base + h if rev else base + h - 1
                pieces.append(-jnp.abs(b_ref[base:base + 2 * h, :] - b_ref[rr:rr + 1, :]))
            ex = jnp.exp2(jnp.concatenate(pieces, axis=0) if len(pieces) > 1 else pieces[0])
            late = ((row & h) == 0) if rev else ((row & h) != 0)
            z = (jnp.where(late, qf, k) * ex).astype(BF16)
            term = mask_ref[mbase + li] * _dot_nt(z, z)
            a = term if a is None else a + term
        pieces = []
        for base in range(0, c, DIAG):
            rr = base + DIAG - 1 if rev else base
            pieces.append(jnp.abs(b_ref[base:base + DIAG, :] - b_ref[rr:rr + 1, :]))
        d = jnp.concatenate(pieces, axis=0)
        qd = (qf * jnp.exp2(-d)).astype(BF16)
        kd = (k * jnp.exp2(d)).astype(BF16)
        a = a + jnp.where(mask_ref[mbase + len(LEVELS)] > 0.5, _dot_nt(qd, kd), 0.0)
        out = _dot(a.astype(BF16), v) + _dot_nt((qf * jnp.exp2(b)).astype(BF16), st.astype(BF16))
    kw = (k * jnp.exp2(b_tot - b)).astype(BF16)
    st_ref[...] = st * jnp.exp2(b_tot) + _dot_tn(v, kw)
    return out


GLA_UNROLL = 4


def _gla_kernel(*refs, nc, with_out):
    if with_out:
        (q_ref, frf_ref, frb_ref, v_ref, lbf_ref, lbb_ref, s0_ref, o_ref, st_out_ref,
         st_ref, b_ref, tri_ref, mask_ref) = refs
    else:
        (frf_ref, frb_ref, v_ref, lbf_ref, lbb_ref, s0_ref, st_out_ref,
         st_ref, b_ref, tri_ref, mask_ref) = refs
        q_ref = o_ref = None
    _gla_fill_masks(tri_ref, mask_ref)
    st_ref[...] = s0_ref[...]
    if with_out:
        o_ref[...] = jnp.zeros(o_ref.shape, o_ref.dtype)
    unroll = math.gcd(nc, GLA_UNROLL)

    def body(ci, carry):
        for u in range(unroll):
            for d, (fr_ref, lb_ref) in enumerate(((frf_ref, lbf_ref), (frb_ref, lbb_ref))):
                step = ci * unroll + u
                cc = step if d == 0 else nc - 1 - step
                rows = pl.ds(pl.multiple_of(cc * CHUNK, CHUNK), CHUNK)
                q = q_ref[rows, :] if with_out else None
                out = _gla_chunk(q, fr_ref[rows, :], v_ref[rows, :], lb_ref[...], st_ref.at[d],
                                 b_ref.at[2 * u + d], tri_ref, mask_ref, d == 1, with_out)
                if with_out:
                    o_ref[rows, :] += out
        return carry

    lax.fori_loop(0, nc // unroll, body, 0)
    st_out_ref[...] = st_ref[...]


def _gla_call(q, fr, v, voff, lb_f, lb_b, s0, with_out=True):
    bn, l, _ = v.shape
    nh = HG_HEADS
    hw = nh * LANES
    seq = lambda off: pl.BlockSpec((None, l, LANES), lambda b, h: (b, 0, h + off))
    lbs = pl.BlockSpec((1, LANES), lambda b, h: (0, h))
    sts = pl.BlockSpec((None, None, 2, LANES, LANES), lambda b, h: (b, h, 0, 0, 0))
    in_specs = [seq(0), seq(nh), seq(voff), lbs, lbs, sts]
    args = [fr, fr, v, lb_f, lb_b, s0]
    out_specs = [sts]
    out_shape = [jax.ShapeDtypeStruct(s0.shape, F32)]
    if with_out:
        in_specs = [seq(0)] + in_specs
        args = [q] + args
        out_specs = [seq(0)] + out_specs
        out_shape = [jax.ShapeDtypeStruct((bn, l, hw), F32)] + out_shape
    res = pl.pallas_call(
        functools.partial(_gla_kernel, nc=l // CHUNK, with_out=with_out),
        grid=(bn, nh),
        in_specs=in_specs,
        out_specs=out_specs,
        out_shape=out_shape,
        scratch_shapes=[pltpu.VMEM((2, LANES, LANES), F32), pltpu.VMEM((2 * GLA_UNROLL, CHUNK, LANES), F32),
                        pltpu.VMEM((2, CHUNK, CHUNK), BF16), pltpu.VMEM((2 * N_GLA_MASKS, CHUNK, CHUNK), F32)],
        compiler_params=_cparams("parallel", "parallel"),
        name="hgrn2",
    )(*args)
    return (res[0], res[1]) if with_out else (None, res[0])


def _softplus(x):
    return jnp.maximum(x, 0.0) + jnp.log1p(jnp.exp(-jnp.abs(x)))


LOG2E = math.log2(math.e)


def _ssd_fill_masks(tri_ref, mask_ref):
    ti = lax.broadcasted_iota(jnp.int32, (CHUNK, CHUNK), 0)
    si = lax.broadcasted_iota(jnp.int32, (CHUNK, CHUNK), 1)
    for d in (0, 1):
        incl = jnp.where((ti <= si) if d else (ti >= si), 1.0, 0.0)
        mask_ref[d] = incl
        tri_ref[d] = incl.astype(BF16)


def _ssd_chunk(xs, bc, cc, cb, dt_col_raw, dt_row_raw, bias_row, bias_col, alog_row, alog_col,
               sn_ref, tri, rev, with_out):
    t = CHUNK
    hp = xs.shape[1]
    npair = hp // LANES
    lane = lax.broadcasted_iota(jnp.int32, (t, LANES), 1)
    first = lane < SSM_HEADDIM

    dt_col = _softplus(dt_col_raw + bias_row)
    a_col = dt_col * (-LOG2E * jnp.exp(alog_row))
    hi, lo = _split2(a_col)
    acum_col = _dot(tri, hi) + _dot(tri, lo)
    if with_out:
        dt_row = _softplus(dt_row_raw + bias_col)
        a_row = dt_row * (-LOG2E * jnp.exp(alog_col))
        hi, lo = _split2(a_row)
        acum_row = _dot_nt(hi, tri) + _dot_nt(lo, tri)

    def pair_cols(m, i):
        return jnp.where(first, m[:, 2 * i:2 * i + 1], m[:, 2 * i + 1:2 * i + 2])

    sn = sn_ref[...]
    ys, xws, dec = [], [], []
    for i in range(npair):
        dt_p = pair_cols(dt_col, i)
        ac_p = pair_cols(acum_col, i)
        xdt = xs[:, i * LANES:(i + 1) * LANES].astype(F32) * dt_p
        atot = ac_p[0:1, :] if rev else ac_p[t - 1:t, :]
        xws.append((xdt * jnp.exp2(atot - ac_p)).astype(BF16))
        dec.append(jnp.exp2(atot))
        if with_out:
            ws = []
            for j in (2 * i, 2 * i + 1):
                diff = jnp.minimum(acum_col[:, j:j + 1] - acum_row[j:j + 1, :], 0.0)
                ws.append((cb * jnp.exp2(diff)).astype(BF16))
            lhs = jnp.concatenate(ws, axis=1)
            zero = jnp.zeros_like(xdt)
            rhs = jnp.concatenate([jnp.where(first, xdt, zero), jnp.where(first, zero, xdt)],
                                  axis=0).astype(BF16)
            y_state = _dot(cc, sn[:, i * LANES:(i + 1) * LANES].astype(BF16)) * jnp.exp2(ac_p)
            ys.append(_dot(lhs, rhs) + y_state)
    xw = jnp.concatenate(xws, axis=1)
    sn_ref[...] = sn * jnp.concatenate(dec, axis=1) + _dot_tn(bc, xw)
    return jnp.concatenate(ys, axis=1) if with_out else None


SSD_UNROLL = 2


def _ssd_kernel(*refs, nc, with_out):
    if with_out:
        (xs_ref, b_ref, c_ref, dtc_ref, dtr_ref, biasr_ref, biasc_ref, alogr_ref, alogc_ref,
         dskip_ref, s0_ref, y_ref, st_out_ref, sn_ref, tri_ref, mask_ref) = refs
    else:
        (xs_ref, b_ref, dtc_ref, biasr_ref, alogr_ref, s0_ref, st_out_ref, sn_ref, tri_ref, mask_ref) = refs
        c_ref = dtr_ref = biasc_ref = alogc_ref = dskip_ref = y_ref = None
    _ssd_fill_masks(tri_ref, mask_ref)
    sn_ref[...] = s0_ref[...]
    unroll = math.gcd(nc, SSD_UNROLL)
    if with_out:
        dsk = dskip_ref[...]

        def init(ci, carry):
            rows = pl.ds(pl.multiple_of(ci * CHUNK, CHUNK), CHUNK)
            y_ref[rows, :] = xs_ref[rows, :].astype(F32) * dsk
            return carry

        lax.fori_loop(0, nc, init, 0)

    def body(ci, carry):
        for u in range(unroll):
            for d in (0, 1):
                step = ci * unroll + u
                cidx = step if d == 0 else nc - 1 - step
                rows = pl.ds(pl.multiple_of(cidx * CHUNK, CHUNK), CHUNK)
                xs = xs_ref[rows, :]
                bc = b_ref[rows, :]
                if with_out:
                    cc = c_ref[rows, :]
                    cb = _dot_nt(cc, bc) * mask_ref[d]
                    y = _ssd_chunk(xs, bc, cc, cb, dtc_ref[d, rows, :], dtr_ref[d, :, rows],
                                   biasr_ref[d], biasc_ref[d], alogr_ref[d], alogc_ref[d],
                                   sn_ref.at[d], tri_ref[d], d == 1, True)
                    y_ref[rows, :] += y
                else:
                    _ssd_chunk(xs, bc, None, None, dtc_ref[d, rows, :], None, biasr_ref[d], None,
                               alogr_ref[d], None, sn_ref.at[d], tri_ref[d], d == 1, False)
        return carry

    lax.fori_loop(0, nc // unroll, body, 0)
    st_out_ref[...] = sn_ref[...]


def _ssd_call(xbc, dt_col, dt_row, bias, alog, dskip, s0, with_out=True):
    bn, l, _ = xbc.shape
    g = SSM_GROUPS
    hg = SSM_HEADS // g
    hp = hg * SSM_HEADDIM
    d_inner = SSM_HEADS * SSM_HEADDIM
    nb_off = d_inner // LANES
    xs_spec = pl.BlockSpec((None, l, hp), lambda b, gi: (b, 0, gi))
    b_spec = pl.BlockSpec((None, l, LANES), lambda b, gi: (b, 0, nb_off + gi))
    c_spec = pl.BlockSpec((None, l, LANES), lambda b, gi: (b, 0, nb_off + g + gi))
    dtc_spec = pl.BlockSpec((None, None, 2, l, hg), lambda b, gi: (b, gi, 0, 0, 0))
    dtr_spec = pl.BlockSpec((None, None, 2, hg, l), lambda b, gi: (b, gi, 0, 0, 0))
    prow = pl.BlockSpec((None, 2, 1, hg), lambda b, gi: (gi, 0, 0, 0))
    pcol = pl.BlockSpec((None, 2, hg, 1), lambda b, gi: (gi, 0, 0, 0))
    dsk_spec = pl.BlockSpec((None, 1, hp), lambda b, gi: (gi, 0, 0))
    st_spec = pl.BlockSpec((None, None, 2, SSM_STATE, hp), lambda b, gi: (b, gi, 0, 0, 0))
    bias_r = bias.reshape(g, 2, 1, hg)
    bias_c = bias.reshape(g, 2, hg, 1)
    alog_r = alog.reshape(g, 2, 1, hg)
    alog_c = alog.reshape(g, 2, hg, 1)
    if with_out:
        in_specs = [xs_spec, b_spec, c_spec, dtc_spec, dtr_spec, prow, pcol, prow, pcol, dsk_spec, st_spec]
        args = [xbc, xbc, xbc, dt_col, dt_row, bias_r, bias_c, alog_r, alog_c, dskip, s0]
        out_specs = [xs_spec, st_spec]
        out_shape = [jax.ShapeDtypeStruct((bn, l, d_inner), F32), jax.ShapeDtypeStruct(s0.shape, F32)]
    else:
        in_specs = [xs_spec, b_spec, dtc_spec, prow, prow, st_spec]
        args = [xbc, xbc, dt_col, bias_r, alog_r, s0]
        out_specs = [st_spec]
        out_shape = [jax.ShapeDtypeStruct(s0.shape, F32)]
    res = pl.pallas_call(
        functools.partial(_ssd_kernel, nc=l // CHUNK, with_out=with_out),
        grid=(bn, g),
        in_specs=in_specs,
        out_specs=out_specs,
        out_shape=out_shape,
        scratch_shapes=[pltpu.VMEM((2, SSM_STATE, hp), F32), pltpu.VMEM((2, CHUNK, CHUNK), BF16),
                        pltpu.VMEM((2, CHUNK, CHUNK), F32)],
        compiler_params=_cparams("parallel", "parallel"),
        name="ssd",
    )(*args)
    return (res[0], res[1]) if with_out else (None, res[0])


def _merge_kernel(x_ref, oatt_ref, ohg_ref, hgate_ref, y_ref, z_ref, gates_ref, hgn_ref, ssn_ref,
                  gm_ref, wba_ref, wbh_ref, wbs_ref, wout_ref, o_ref):
    d = x_ref.shape[1]
    hgn = hgn_ref[...]
    parts = []
    for s in range(ohg_ref.shape[1] // LANES):
        sl = slice(s * LANES, (s + 1) * LANES)
        o = ohg_ref[:, sl]
        o = o * lax.rsqrt(jnp.mean(o * o, axis=1, keepdims=True) + EPS) * hgn
        parts.append((o * _silu(hgate_ref[:, sl].astype(F32))).astype(BF16))
    o_hg = jnp.concatenate(parts, axis=1)
    ys = y_ref[...] * _silu(z_ref[...].astype(F32))
    o_ssm = (ys * lax.rsqrt(jnp.mean(ys * ys, axis=1, keepdims=True) + EPS) * ssn_ref[...]).astype(BF16)
    g_att = _sigmoid(gates_ref[:, 0:d].astype(F32))
    g_hg = _sigmoid(gates_ref[:, d:2 * d].astype(F32))
    g_ssm = _sigmoid(gates_ref[:, 2 * d:3 * d].astype(F32))
    m = (g_att * _dot(oatt_ref[...], wba_ref[...]) + g_hg * _dot(o_hg, wbh_ref[...])
         + g_ssm * _dot(o_ssm, wbs_ref[...]))
    mix = _dot(m.astype(BF16), wout_ref[...])
    o_ref[...] = x_ref[...] + gm_ref[...] * mix


def _merge_call(x, o_att, o_hg, y, gz, hg_norm_g, ssm_norm_g, g_m, wba, wbh, wbs, wout):
    bn, l, d = x.shape
    tm = min(l, DENSE_TM)
    di = y.shape[2]
    assert di == 2 * d and gz.shape[2] == 6 * d
    tok = lambda w, j=0: pl.BlockSpec((None, tm, w), lambda b, i: (b, i, j))
    full = lambda a: pl.BlockSpec(a.shape, lambda b, i: (0,) * a.ndim, pipeline_mode=pl.Buffered(1))
    hgn = hg_norm_g.reshape(1, LANES)
    ssn = ssm_norm_g.reshape(1, di)
    return pl.pallas_call(
        _merge_kernel,
        grid=(bn, l // tm),
        in_specs=[tok(d), tok(d), tok(d), tok(d, 2), tok(di), tok(di, 0), tok(N_BRANCH * d, 1), full(hgn), full(ssn),
                  pl.BlockSpec((None, 1, d), lambda b, i: (b, 0, 0)),
                  full(wba), full(wbh), full(wbs), full(wout)],
        out_specs=tok(d),
        out_shape=jax.ShapeDtypeStruct((bn, l, d), F32),
        compiler_params=_cparams("parallel", "parallel"),
        name="merge",
    )(x, o_att, o_hg, gz, y, gz, gz, hgn, ssn, g_m.reshape(bn, 1, d), wba, wbh, wbs, wout)


def _ffn_kernel(x_ref, g_ref, sh_ref, sc_ref, gf_ref, w1_ref, w3_ref, w2_ref, fin_ref, o_ref, *, final):
    x = x_ref[...]
    h = x * lax.rsqrt(jnp.mean(x * x, axis=-1, keepdims=True) + EPS) * g_ref[...]
    h = (h * (1.0 + sc_ref[...]) + sh_ref[...]).astype(BF16)
    u = _dot(h, w1_ref[...])
    t = _dot(h, w3_ref[...])
    a = (_silu(u) * t).astype(BF16)
    y = x + gf_ref[...] * _dot(a, w2_ref[...])
    if final:
        y = y * lax.rsqrt(jnp.mean(y * y, axis=-1, keepdims=True) + EPS) * fin_ref[...]
    o_ref[...] = y


def _ffn_call(x, g, shift, scale, gate, w1, w3, w2, final_g, final):
    bn, l, d = x.shape
    tm = min(l, DENSE_TM)
    tok = pl.BlockSpec((None, tm, d), lambda b, i: (b, i, 0))
    vec = pl.BlockSpec((None, 1, d), lambda b, i: (b, 0, 0))
    row = pl.BlockSpec((1, d), lambda b, i: (0, 0))
    full = lambda a: pl.BlockSpec(a.shape, lambda b, i: (0, 0), pipeline_mode=pl.Buffered(1))
    return pl.pallas_call(
        functools.partial(_ffn_kernel, final=final),
        grid=(bn, l // tm),
        in_specs=[tok, row, vec, vec, vec, full(w1), full(w3), full(w2), row],
        out_specs=tok,
        out_shape=jax.ShapeDtypeStruct((bn, l, d), F32),
        compiler_params=_cparams("parallel", "parallel"),
        name="ffn",
    )(x, g.reshape(1, d), shift.reshape(bn, 1, d), scale.reshape(bn, 1, d), gate.reshape(bn, 1, d),
      w1, w3, w2, final_g.reshape(1, d))


def _rope_perm():
    idx = []
    for lane in range(LANES):
        half, mp, axis, i = lane // 64, (lane // 32) % 2, (lane // 16) % 2, lane % 16
        idx.append(mp * 64 + axis * 32 + half * 16 + i)
    return idx


def _rope_tables(l):
    rows = l // GRID_W
    row = jnp.repeat(jnp.arange(rows, dtype=F32), GRID_W)
    col = jnp.broadcast_to(jnp.arange(GRID_W, dtype=F32), (rows, GRID_W)).reshape(-1)
    nfreq = ATT_HD // 4
    freqs = ROPE_THETA ** (-jnp.arange(nfreq, dtype=F32) / nfreq)
    ang_r = row[:, None] * freqs[None, :]
    ang_c = col[:, None] * freqs[None, :]
    ang = jnp.concatenate([ang_r, ang_c, ang_r, ang_c], axis=1)
    ang = jnp.concatenate([ang, ang], axis=1)
    sign = jnp.where(jnp.arange(LANES) < LANES // 2, -1.0, 1.0).astype(F32)
    return jnp.cos(ang), jnp.sin(ang) * sign[None, :]


def _dt_layouts(dt_raw, bn, l):
    hg = SSM_HEADS // SSM_GROUPS
    t = dt_raw.reshape(bn, l, 2, SSM_GROUPS, hg)
    return t.transpose(0, 3, 2, 1, 4), t.transpose(0, 3, 2, 4, 1)


def kernel(x, c, ctx, c_ctx, w_ada, b_ada, norm1_g, w_in, att_lambda, att_norm_g, hg_lb_logits, hg_norm_g,
           ssm_conv_w, ssm_conv_b, ssm_dt_bias, ssm_a_log, ssm_d, ssm_norm_g, w_branch_att, w_branch_hg,
           w_branch_ssm, w_out, norm2_g, ffn_w1, ffn_w3, ffn_w2, final_g):
    bn, l, d = x.shape
    lc = ctx.shape[1]
    depth = w_ada.shape[0]
    att_w = ATT_HEADS * 2 * ATT_HD
    hg_w = HG_HEADS * LANES
    d_inner = SSM_HEADS * SSM_HEADDIM
    xbc_w = d_inner + 2 * SSM_GROUPS * SSM_STATE
    hgrp = SSM_HEADS // SSM_GROUPS

    sizes = (att_w, att_w, hg_w, hg_w, hg_w, xbc_w, SSM_HEADS, SSM_HEADS, att_w, hg_w, hg_w, d_inner, N_BRANCH * d)
    offs = [0]
    for s in sizes:
        offs.append(offs[-1] + s)
    (o_ak, o_av, o_hff, o_hfb, o_hi, o_xbc, o_dtf, o_dtb, o_aq, o_hq, o_hgate, o_z, o_gates, _) = offs

    perm = jnp.asarray([h * LANES + p for h in range(ATT_HEADS) for p in _rope_perm()], jnp.int32)
    cos_t, sin_t = _rope_tables(l)
    q_scale = ATT_HD ** -0.5 * math.log2(math.e)

    lb_w = jax.nn.softmax(hg_lb_logits.astype(F32), axis=1)
    lb = jnp.cumsum(lb_w, axis=1) - lb_w[:, :1]

    pad_rows = (-(bn + 1)) % SUBLANES
    c_rows = jnp.concatenate([c, c_ctx[None, :], jnp.zeros((pad_rows, d), F32)], axis=0)
    mod_all = _mod_call(c_rows, w_ada, b_ada)

    xc = ctx
    for li in range(depth):
        last = li == depth - 1
        mod = mod_all[li, :bn]
        mod_c = jnp.broadcast_to(mod_all[li, bn:bn + 1], (bn, 6 * d))
        sh_m, sc_m, g_m, sh_f, sc_f, g_f = [mod[:, i * d:(i + 1) * d] for i in range(6)]
        shc_m, scc_m, gc_m, shc_f, scc_f, gc_f = [mod_c[:, i * d:(i + 1) * d] for i in range(6)]

        lam_init = 0.8 - 0.6 * math.exp(-0.3 * li)
        al = att_lambda[li].astype(F32)
        lam = (jnp.exp(jnp.sum(al[0] * al[1])) - jnp.exp(jnp.sum(al[2] * al[3])) + lam_init).reshape(1)

        w = w_in[li]
        wb = lambda o, n: w[:, o:o + n].astype(BF16)
        w_q = wb(o_aq, att_w)[:, perm]
        w_k = wb(o_ak, att_w)[:, perm]
        w_vt = wb(o_av, att_w).T
        w_hi = wb(o_hi, hg_w)
        w_f = jnp.concatenate([wb(o_hff, 2 * hg_w), wb(o_dtf, 2 * SSM_HEADS),
                               jnp.zeros((d, LANES - 2 * SSM_HEADS), BF16)], axis=1)
        w_hq = wb(o_hq, hg_w)
        w_xbc = wb(o_xbc, xbc_w)
        w_gz = jnp.concatenate([wb(o_z, d_inner), wb(o_hgate, hg_w), wb(o_gates, N_BRANCH * d)], axis=1)

        lb_f = lb[0, li].reshape(1, hg_w)
        lb_b = lb[1, li].reshape(1, hg_w)
        bias_g = ssm_dt_bias[li].reshape(2, SSM_GROUPS, hgrp).transpose(1, 0, 2)
        alog_g = ssm_a_log[li].astype(F32).reshape(2, SSM_GROUPS, hgrp).transpose(1, 0, 2)
        dskip = jnp.repeat(ssm_d[li], SSM_HEADDIM).reshape(SSM_GROUPS, 1, hgrp * SSM_HEADDIM)
        wba = w_branch_att[li].astype(BF16)
        wbh = w_branch_hg[li].astype(BF16)
        wbs = w_branch_ssm[li].astype(BF16)
        wo = w_out[li].astype(BF16)
        w1 = ffn_w1[li].astype(BF16)
        w3 = ffn_w3[li].astype(BF16)
        w2 = ffn_w2[li].astype(BF16)

        hc = _adaln_call(xc, norm1_g[li], shc_m, scc_m)
        kc = _proj_call(hc, w_k, BF16, name="proj_kc")
        vtc = _projt_call(hc, w_vt, BF16, name="proj_vtc")
        vhc = _proj_call(hc, w_hi, BF16, name="proj_vhc")
        frc = _proj_call(hc, w_f, F32, tn=w_f.shape[1], name="proj_fc")
        xbcc = _conv_call(_proj_call(hc, w_xbc, BF16, name="proj_xbcc"), ssm_conv_w[li], ssm_conv_b[li])
        dtc = frc[:, :, 2 * hg_w:2 * hg_w + 2 * SSM_HEADS]
        dtc_col, dtc_row = _dt_layouts(dtc, bn, lc)
        zero_g = jnp.zeros((bn, HG_HEADS, 2, LANES, LANES), F32)
        zero_s = jnp.zeros((bn, SSM_GROUPS, 2, SSM_STATE, hgrp * SSM_HEADDIM), F32)
        if last:
            _, sg_ctx = _gla_call(None, frc, vhc, 0, lb_f, lb_b, zero_g, with_out=False)
            _, ss_ctx = _ssd_call(xbcc, dtc_col, dtc_row, bias_g, alog_g, dskip, zero_s, with_out=False)
        else:
            qc = _proj_call(hc, w_q, BF16, epi="scale", scale=q_scale, name="proj_qc")
            hqc = _proj_call(hc, w_hq, BF16, epi="silu", name="proj_hqc")
            gzc = _proj_call(hc, w_gz, BF16, name="proj_gzc")
            oc_hg, sg_ctx = _gla_call(hqc, frc, vhc, 0, lb_f, lb_b, zero_g)
            yc, ss_ctx = _ssd_call(xbcc, dtc_col, dtc_row, bias_g, alog_g, dskip, zero_s)
            oc_att = _attn_call(lam, qc, [(kc, vtc)], att_norm_g[li], 1.0 - lam_init)

        h = _adaln_call(x, norm1_g[li], sh_m, sc_m)
        q = _proj_call(h, w_q, BF16, epi="rope", tables=(cos_t * q_scale, sin_t * q_scale), name="proj_q")
        k = _proj_call(h, w_k, BF16, epi="rope", tables=(cos_t, sin_t), name="proj_k")
        vt = _projt_call(h, w_vt, BF16, name="proj_vt")
        vh = _proj_call(h, w_hi, BF16, name="proj_vh")
        fr = _proj_call(h, w_f, F32, tn=w_f.shape[1], name="proj_f")
        hq = _proj_call(h, w_hq, BF16, epi="silu", name="proj_hq")
        xbc = _conv_call(_proj_call(h, w_xbc, BF16, name="proj_xbc"), ssm_conv_w[li], ssm_conv_b[li])
        dt = fr[:, :, 2 * hg_w:2 * hg_w + 2 * SSM_HEADS]
        gz = _proj_call(h, w_gz, BF16, name="proj_gz")
        dt_col, dt_row = _dt_layouts(dt, bn, l)

        o_att = _attn_call(lam, q, [(kc, vtc), (k, vt)], att_norm_g[li], 1.0 - lam_init)
        o_hg, _ = _gla_call(hq, fr, vh, 0, lb_f, lb_b, sg_ctx)
        y, _ = _ssd_call(xbc, dt_col, dt_row, bias_g, alog_g, dskip, ss_ctx)

        x = _merge_call(x, o_att, o_hg, y, gz, hg_norm_g[li], ssm_norm_g[li], g_m, wba, wbh, wbs, wo)
        x = _ffn_call(x, norm2_g[li], sh_f, sc_f, g_f, w1, w3, w2, final_g, final=last)

        if not last:
            xc = _merge_call(xc, oc_att, oc_hg, yc, gzc, hg_norm_g[li], ssm_norm_g[li], gc_m,
                             wba, wbh, wbs, wo)
            xc = _ffn_call(xc, norm2_g[li], shc_f, scc_f, gc_f, w1, w3, w2, final_g, final=False)
    return x
```
